```python
import jax, jax.numpy as jnp
from jax import lax
import numpy as np

D_MODEL = 1024
BATCH = 4
SEQ = 4096
DEPTH = 4

MEM_LEN = 256
MLA_HEADS = 8
MLA_NOPE = 64
MLA_ROPE = 32
MLA_V = 64
Q_LORA = 384
KV_LORA = 256
ROPE_THETA = 10000.0
Q_BLOCK = 128
HG_HEADS = 4
HG_KDIM = 128
HG_VDIM = 128
HG_CHUNK = 64
MEM_HEADS = 4
MEM_HDIM = 128
D_FF = 2816
N_BRANCH = 3
NORM_EPS = 1e-6

MLA_QK = MLA_NOPE + MLA_ROPE
HG_K = HG_HEADS * HG_KDIM
HG_V = HG_HEADS * HG_VDIM
MEM_W = MEM_HEADS * MEM_HDIM
IN_SPLITS = (Q_LORA, KV_LORA, MLA_ROPE, HG_K, HG_K, HG_V, HG_V, MEM_W, N_BRANCH * D_MODEL)
D_IN = sum(IN_SPLITS)

kernel_name = "hybrid_mla_hgrn2_memory_macaron"


def rmsnorm(x, g):
    xf = x.astype(jnp.float32)
    y = xf * lax.rsqrt(jnp.mean(xf * xf, axis=-1, keepdims=True) + NORM_EPS)
    return (y * g.astype(jnp.float32)).astype(x.dtype)


def swiglu(x, w_in, w_out):
    a, b = jnp.split(x @ w_in, 2, axis=-1)
    return (jax.nn.silu(a) * b) @ w_out


def split_sizes(z, sizes):
    idx = [int(v) for v in np.cumsum(sizes)[:-1]]
    return jnp.split(z, idx, axis=-1)


def apply_rope(t, cos, sin):
    tf = t.astype(jnp.float32)
    t1, t2 = jnp.split(tf, 2, axis=-1)
    out = jnp.concatenate([t1 * cos - t2 * sin, t2 * cos + t1 * sin], axis=-1)
    return out.astype(t.dtype)


def causal_mla_attention(q_nope, q_rope, k_nope, k_rope, v):
    B, S, H, _ = q_nope.shape
    nb = S // Q_BLOCK
    scale = MLA_QK ** -0.5
    k_pos = jnp.arange(S)

    def to_blocks(t):
        return jnp.moveaxis(t.reshape((B, nb, Q_BLOCK) + t.shape[2:]), 1, 0)

    def block(args):
        i, qn, qr = args
        s = jnp.einsum('bqhd,bkhd->bhqk', qn, k_nope) + jnp.einsum('bqhd,bkd->bhqk', qr, k_rope)
        s = s.astype(jnp.float32) * scale
        q_pos = i * Q_BLOCK + jnp.arange(Q_BLOCK)
        s = jnp.where(q_pos[:, None] >= k_pos[None, :], s, -jnp.inf)
        p = jax.nn.softmax(s, axis=-1).astype(v.dtype)
        return jnp.einsum('bhqk,bkhd->bqhd', p, v)

    o = lax.map(block, (jnp.arange(nb), to_blocks(q_nope), to_blocks(q_rope)))
    return jnp.moveaxis(o, 0, 1).reshape(B, S, H * MLA_V)


def mla_branch(c_q, c_kv, k_r, cos, sin, q_norm_g, kv_norm_g, w_uq, w_uk, w_uv):
    B, S, _ = c_q.shape
    q = (rmsnorm(c_q, q_norm_g) @ w_uq).reshape(B, S, MLA_HEADS, MLA_QK)
    q_nope = q[..., :MLA_NOPE]
    q_rope = apply_rope(q[..., MLA_NOPE:], cos[:, :, None, :], sin[:, :, None, :])
    k_rope = apply_rope(k_r, cos, sin)
    ckv = rmsnorm(c_kv, kv_norm_g)
    k_nope = (ckv @ w_uk).reshape(B, S, MLA_HEADS, MLA_NOPE)
    v = (ckv @ w_uv).reshape(B, S, MLA_HEADS, MLA_V)
    return causal_mla_attention(q_nope, q_rope, k_nope, k_rope, v)


def chunk_gated_recurrence(q, k, v, g):
    B, S, H, K = q.shape
    V = v.shape[-1]
    n = S // HG_CHUNK
    causal = jnp.tril(jnp.ones((HG_CHUNK, HG_CHUNK), dtype=bool))

    def chunks(t):
        return t.reshape(B, n, HG_CHUNK, H, t.shape[-1]).transpose(1, 0, 3, 2, 4)

    def step(state, inp):
        qc, kc, vc, gc = inp
        G = jnp.cumsum(gc, axis=2)
        o_inter = jnp.einsum('bhck,bhkv->bhcv', qc * jnp.exp(G), state)
        diff = G[:, :, :, None, :] - G[:, :, None, :, :]
        decay = jnp.exp(jnp.where(causal[:, :, None], diff, -jnp.inf))
        A = jnp.einsum('bhtsk,bhsk->bhts', qc[:, :, :, None, :] * decay, kc)
        o_intra = jnp.einsum('bhts,bhsv->bhtv', A, vc)
        G_last = G[:, :, -1:, :]
        new_state = jnp.exp(G_last[:, :, 0, :])[..., None] * state + jnp.einsum(
            'bhck,bhcv->bhkv', kc * jnp.exp(G_last - G), vc)
        return new_state, o_inter + o_intra

    s0 = jnp.zeros((B, H, K, V), jnp.float32)
    _, o = lax.scan(step, s0, (chunks(q), chunks(k), chunks(v), chunks(g)))
    return o.transpose(1, 0, 3, 2, 4).reshape(B, S, H, V)


def hgrn2_branch(q, f_logit, i_in, gate, lb, o_norm_g):
    B, S, _ = q.shape
    dt = q.dtype
    qf = jax.nn.silu(q.astype(jnp.float32)).reshape(B, S, HG_HEADS, HG_KDIM)
    z = f_logit.astype(jnp.float32)
    g = jnp.logaddexp(jnp.log(lb), jnp.log1p(-lb) + jax.nn.log_sigmoid(z))
    k = (1.0 - lb) * jax.nn.sigmoid(-z)
    g = g.reshape(B, S, HG_HEADS, HG_KDIM)
    k = k.reshape(B, S, HG_HEADS, HG_KDIM)
    v = i_in.astype(jnp.float32).reshape(B, S, HG_HEADS, HG_VDIM)
    o = chunk_gated_recurrence(qf, k, v, g)
    o = rmsnorm(o, o_norm_g) * jax.nn.silu(gate.astype(jnp.float32).reshape(B, S, HG_HEADS, HG_VDIM))
    return o.reshape(B, S, HG_V).astype(dt)


def memory_branch(q, mem_n, w_mem_kv):
    B, S, _ = q.shape
    M = mem_n.shape[1]
    qh = q.reshape(B, S, MEM_HEADS, MEM_HDIM)
    k, v = jnp.split(mem_n @ w_mem_kv, 2, axis=-1)
    k = k.reshape(B, M, MEM_HEADS, MEM_HDIM)
    v = v.reshape(B, M, MEM_HEADS, MEM_HDIM)
    s = jnp.einsum('bqhd,bmhd->bhqm', qh, k).astype(jnp.float32) * (MEM_HDIM ** -0.5)
    p = jax.nn.softmax(s, axis=-1).astype(v.dtype)
    return jnp.einsum('bhqm,bmhd->bqhd', p, v).reshape(B, S, MEM_W)


def setup_inputs(seed: int = 0) -> dict:
    key = jax.random.key(seed)
    ks = jax.random.split(key, 32)
    f32 = jnp.float32

    def w(k, shape, fan_in):
        return jax.random.normal(k, shape, f32) * (fan_in ** -0.5)

    def gain(k, shape):
        return 1.0 + 0.02 * jax.random.normal(k, shape, f32)

    x = jax.random.normal(ks[0], (BATCH, SEQ, D_MODEL), f32)
    mem = jax.random.normal(ks[1], (BATCH, MEM_LEN, D_MODEL), f32)
    offsets = jax.random.randint(ks[2], (BATCH, 1), 0, 1024, dtype=jnp.int32)
    positions = offsets + jnp.arange(SEQ, dtype=jnp.int32)[None, :]
    return {
        "x": x,
        "mem": mem,
        "positions": positions,
        "ffn1_norm": gain(ks[3], (DEPTH, D_MODEL)),
        "w_ffn1_in": w(ks[4], (DEPTH, D_MODEL, 2 * D_FF), D_MODEL),
        "w_ffn1_out": w(ks[5], (DEPTH, D_FF, D_MODEL), D_FF),
        "mix_norm": gain(ks[6], (DEPTH, D_MODEL)),
        "w_in": w(ks[7], (DEPTH, D_MODEL, D_IN), D_MODEL),
        "q_lat_norm": gain(ks[8], (DEPTH, Q_LORA)),
        "kv_lat_norm": gain(ks[9], (DEPTH, KV_LORA)),
        "w_uq": w(ks[10], (DEPTH, Q_LORA, MLA_HEADS * MLA_QK), Q_LORA),
        "w_uk": w(ks[11], (DEPTH, KV_LORA, MLA_HEADS * MLA_NOPE), KV_LORA),
        "w_uv": w(ks[12], (DEPTH, KV_LORA, MLA_HEADS * MLA_V), KV_LORA),
        "w_o_mla": w(ks[13], (DEPTH, MLA_HEADS * MLA_V, D_MODEL), MLA_HEADS * MLA_V),
        "hg_lower_bounds": jax.random.normal(ks[14], (DEPTH, HG_K), f32),
        "hg_out_norm": gain(ks[15], (DEPTH, HG_VDIM)),
        "w_o_hg": w(ks[16], (DEPTH, HG_V, D_MODEL), HG_V),
        "mem_norm": gain(ks[17], (DEPTH, D_MODEL)),
        "w_mem_kv": w(ks[18], (DEPTH, D_MODEL, 2 * MEM_W), D_MODEL),
        "w_o_mem": w(ks[19], (DEPTH, MEM_W, D_MODEL), MEM_W),
        "w_out": w(ks[20], (DEPTH, D_MODEL, D_MODEL), D_MODEL),
        "ffn2_norm": gain(ks[21], (DEPTH, D_MODEL)),
        "w_ffn2_in": w(ks[22], (DEPTH, D_MODEL, 2 * D_FF), D_MODEL),
        "w_ffn2_out": w(ks[23], (DEPTH, D_FF, D_MODEL), D_FF),
        "final_norm": gain(ks[24], (D_MODEL,)),
    }


def reference(x, mem, positions, ffn1_norm, w_ffn1_in, w_ffn1_out, mix_norm, w_in,
              q_lat_norm, kv_lat_norm, w_uq, w_uk, w_uv, w_o_mla, hg_lower_bounds,
              hg_out_norm, w_o_hg, mem_norm, w_mem_kv, w_o_mem, w_out, ffn2_norm,
              w_ffn2_in, w_ffn2_out, final_norm):
    B, S, D = x.shape
    inv_freq = ROPE_THETA ** (-jnp.arange(0, MLA_ROPE, 2, dtype=jnp.float32) / MLA_ROPE)
    ang = positions.astype(jnp.float32)[..., None] * inv_freq
    cos, sin = jnp.cos(ang), jnp.sin(ang)
    lbs = jnp.cumsum(jax.nn.softmax(hg_lower_bounds.astype(jnp.float32), axis=0), axis=0)
    lbs = lbs - lbs[0:1]

    for l in range(DEPTH):
        x = x + 0.5 * swiglu(rmsnorm(x, ffn1_norm[l]), w_ffn1_in[l], w_ffn1_out[l])
        u = rmsnorm(x, mix_norm[l])
        c_q, c_kv, k_r, hq, hf, hi, hgate, mq, gate_logits = split_sizes(u @ w_in[l], IN_SPLITS)
        y_mla = mla_branch(c_q, c_kv, k_r, cos, sin, q_lat_norm[l], kv_lat_norm[l],
                           w_uq[l], w_uk[l], w_uv[l]) @ w_o_mla[l]
        y_hg = hgrn2_branch(hq, hf, hi, hgate, lbs[l], hg_out_norm[l]) @ w_o_hg[l]
        mem_n = rmsnorm(mem, mem_norm[l])
        y_mem = memory_branch(mq, mem_n, w_mem_kv[l]) @ w_o_mem[l]
        gates = jax.nn.sigmoid(gate_logits.reshape(B, S, N_BRANCH, D))
        merged = gates[:, :, 0] * y_mla + gates[:, :, 1] * y_hg + gates[:, :, 2] * y_mem
        x = x + merged @ w_out[l]
        x = x + 0.5 * swiglu(rmsnorm(x, ffn2_norm[l]), w_ffn2_in[l], w_ffn2_out[l])

    return rmsnorm(x, final_norm)
```

```python
import functools

import numpy as np
import jax
import jax.numpy as jnp
from jax import lax
from jax.experimental import pallas as pl
from jax.experimental.pallas import tpu as pltpu

F32 = jnp.float32
BF16 = jnp.bfloat16

MLA_HEADS = 8
MLA_NOPE = 64
MLA_ROPE = 32
MLA_V = 64
Q_LORA = 384
KV_LORA = 256
ROPE_THETA = 10000.0
HG_HEADS = 4
HG_KDIM = 128
HG_VDIM = 128
MEM_HEADS = 4
MEM_HDIM = 128
N_BRANCH = 3
NORM_EPS = 1e-6
MLA_QK = MLA_NOPE + MLA_ROPE
HG_W = HG_HEADS * HG_KDIM
MEM_W = MEM_HEADS * MEM_HDIM

LANES = 128
SUBLANES = 8
VMEM_LIMIT_BYTES = 56 * 1024 * 1024

HEAD_PAD = LANES
ROPE_HALF = MLA_ROPE // 2
NEG_BIG = -1e30


def _rms(x, g):
    ms = jnp.mean(x * x, axis=-1, keepdims=True)
    return x * lax.rsqrt(ms + NORM_EPS) * g


def _sigmoid(x):
    return 1.0 / (1.0 + jnp.exp(-x))


def _dot(a, b):
    return jnp.dot(a, b, preferred_element_type=F32)


def _dot_nt(a, b):
    return lax.dot_general(a, b, (((1,), (1,)), ((), ())), preferred_element_type=F32)


def _dot_tn(a, b):
    return lax.dot_general(a, b, (((0,), (0,)), ((), ())), preferred_element_type=F32)


def _const_spec(shape):
    nd = len(shape)
    return pl.BlockSpec(shape, lambda *_: (0,) * nd, pipeline_mode=pl.Buffered(1))


def _params(sem):
    return pltpu.CompilerParams(dimension_semantics=sem, vmem_limit_bytes=VMEM_LIMIT_BYTES)


def _rope_table_kernel(pos_ref, freq_ref, cf_ref, s1_ref, s2_ref):
    ang = pos_ref[...].astype(F32) * freq_ref[...]
    c = jnp.cos(ang)
    s = jnp.sin(ang)
    lane = lax.broadcasted_iota(jnp.int32, ang.shape, 1)
    lo = (lane >= MLA_NOPE) & (lane < MLA_NOPE + ROPE_HALF)
    hi = (lane >= MLA_NOPE + ROPE_HALF) & (lane < MLA_QK)
    cf_ref[...] = jnp.where(lane < MLA_NOPE, 1.0, jnp.where(lane < MLA_QK, c, 0.0))
    s1_ref[...] = jnp.where(lo, -s, 0.0)
    s2_ref[...] = jnp.where(hi, s, 0.0)


def _rope_tables(positions):
    T = positions.size
    tm = min(T, 1024)
    inv = ROPE_THETA ** (-jnp.arange(0, MLA_ROPE, 2, dtype=F32) / MLA_ROPE)
    freq = jnp.concatenate([jnp.zeros((MLA_NOPE,), F32), inv, inv,
                            jnp.zeros((LANES - MLA_QK,), F32)]).reshape(1, LANES)
    out = jax.ShapeDtypeStruct((T, LANES), F32)
    row = pl.BlockSpec((tm, LANES), lambda i: (i, 0))
    return pl.pallas_call(
        _rope_table_kernel,
        out_shape=(out, out, out),
        grid=(T // tm,),
        in_specs=[pl.BlockSpec((tm, 1), lambda i: (i, 0)), _const_spec((1, LANES))],
        out_specs=(row, row, row),
        compiler_params=_params(("parallel",)),
        name="rope_tables",
    )(positions.reshape(T, 1), freq)


def _lower_bound_kernel(x_ref, loglb_ref, log1mlb_ref, onemlb_ref):
    x = x_ref[...]
    depth = x.shape[0]
    e = jnp.exp(x - jnp.max(x, axis=0, keepdims=True))
    p = e / jnp.sum(e, axis=0, keepdims=True)
    acc = jnp.zeros_like(p[0:1])
    for l in range(depth):
        if l > 0:
            acc = acc + p[l:l + 1]
        loglb_ref[l:l + 1, :] = jnp.log(acc)
        log1mlb_ref[l:l + 1, :] = jnp.log1p(-acc)
        onemlb_ref[l:l + 1, :] = 1.0 - acc


def _lower_bounds(hg_lower_bounds):
    out = jax.ShapeDtypeStruct(hg_lower_bounds.shape, F32)
    return pl.pallas_call(_lower_bound_kernel, out_shape=(out, out, out),
                          name="hgrn_lower_bounds")(hg_lower_bounds.astype(F32))


def _memkv_kernel(mem_ref, g_ref, wkt_ref, wv_ref, kt_ref, v_ref):
    mn = _rms(mem_ref[...], g_ref[...]).astype(BF16)
    kt_ref[...] = _dot_nt(wkt_ref[...], mn).astype(BF16)
    v_ref[...] = _dot(mn, wv_ref[...]).astype(BF16)


def _memkv(mem, mem_norm, wkt, wv):
    B, M, D = mem.shape
    L = mem_norm.shape[0]
    return pl.pallas_call(
        _memkv_kernel,
        out_shape=(jax.ShapeDtypeStruct((L, B, MEM_W, M), BF16),
                   jax.ShapeDtypeStruct((L, B, M, MEM_W), BF16)),
        grid=(L, B),
        in_specs=[pl.BlockSpec((None, M, D), lambda l, b: (b, 0, 0)),
                  pl.BlockSpec((None, 1, D), lambda l, b: (l, 0, 0)),
                  pl.BlockSpec((None, MEM_W, D), lambda l, b: (l, 0, 0)),
                  pl.BlockSpec((None, D, MEM_W), lambda l, b: (l, 0, 0))],
        out_specs=(pl.BlockSpec((None, None, MEM_W, M), lambda l, b: (l, b, 0, 0)),
                   pl.BlockSpec((None, None, M, MEM_W), lambda l, b: (l, b, 0, 0))),
        compiler_params=_params(("parallel", "parallel")),
        name="mem_kv",
    )(mem, mem_norm.reshape(L, 1, D), wkt, wv)


def _ffn_kernel(x_ref, g_ref, w1_ref, w2_ref, gf_ref, o_ref, *, d_ff, tf, final_norm):
    x = x_ref[...]
    xn = _rms(x, g_ref[...]).astype(BF16)
    acc = None
    for c in range(d_ff // tf):
        a = _dot(xn, w1_ref[:, c * tf:(c + 1) * tf])
        b = _dot(xn, w1_ref[:, d_ff + c * tf:d_ff + (c + 1) * tf])
        h = (a * _sigmoid(a) * b).astype(BF16)
        part = _dot(h, w2_ref[c * tf:(c + 1) * tf, :])
        acc = part if acc is None else acc + part
    y = x + 0.5 * acc
    if final_norm:
        y = _rms(y, gf_ref[...])
    o_ref[...] = y


def _ffn(x, g, w1, w2, gf, *, final_norm, tm=512, tf=1408):
    T, D = x.shape
    d_ff = w2.shape[0]
    tm = min(tm, T)
    row = pl.BlockSpec((tm, D), lambda i: (i, 0))
    return pl.pallas_call(
        functools.partial(_ffn_kernel, d_ff=d_ff, tf=tf, final_norm=final_norm),
        out_shape=jax.ShapeDtypeStruct((T, D), F32),
        grid=(T // tm,),
        in_specs=[row, _const_spec((1, D)), _const_spec((D, 2 * d_ff)), _const_spec((d_ff, D)),
                  _const_spec((1, D))],
        out_specs=row,
        compiler_params=_params(("parallel",)),
        name="ffn",
    )(x, g.reshape(1, D), w1, w2, gf.reshape(1, D))


def _proj_kernel(x_ref, g_ref, wa_ref, wh_ref, wm_ref, wg_ref, qn_ref, kvn_ref, wuq_ref, wuk_ref,
                 wuv_ref, cf_ref, s1_ref, s2_ref, kt_ref, vm_ref, wom_ref,
                 q_ref, k_ref, v_ref, h_ref, g01_ref, pm_ref):
    D = x_ref.shape[1]
    u = _rms(x_ref[...], g_ref[...]).astype(BF16)
    cf = cf_ref[...]
    s1 = s1_ref[...]
    s2 = s2_ref[...]

    def rope(t):
        return (t * cf + pltpu.roll(t, LANES - ROPE_HALF, axis=1) * s1
                + pltpu.roll(t, ROPE_HALF, axis=1) * s2)

    za = _dot(u, wa_ref[...])
    cq = _rms(za[:, :Q_LORA], qn_ref[...]).astype(BF16)
    ckv = _rms(za[:, Q_LORA:Q_LORA + KV_LORA], kvn_ref[...]).astype(BF16)
    kr = rope(za[:, Q_LORA + KV_LORA:])
    q = _dot(cq, wuq_ref[...])
    k = _dot(ckv, wuk_ref[...])
    for h in range(MLA_HEADS):
        sl = slice(h * HEAD_PAD, (h + 1) * HEAD_PAD)
        q_ref[:, sl] = rope(q[:, sl]).astype(BF16)
        k_ref[:, sl] = (k[:, sl] + kr).astype(BF16)
    v_ref[...] = _dot(ckv, wuv_ref[...]).astype(BF16)

    for c in range(4):
        sl = slice(c * HG_W, (c + 1) * HG_W)
        h_ref[:, sl] = _dot(u, wh_ref[:, sl])

    mq = _dot(u, wm_ref[...]).astype(BF16)
    heads = []
    for h in range(MEM_HEADS):
        sl = slice(h * MEM_HDIM, (h + 1) * MEM_HDIM)
        s = _dot(mq[:, sl], kt_ref[sl, :])
        p = jnp.exp(s - jnp.max(s, axis=-1, keepdims=True))
        o = _dot(p.astype(BF16), vm_ref[:, sl])
        heads.append((o / jnp.sum(p, axis=-1, keepdims=True)).astype(BF16))
    y_mem = _dot(jnp.concatenate(heads, axis=1), wom_ref[...])

    for b in range(N_BRANCH):
        gate = _sigmoid(_dot(u, wg_ref[:, b * D:(b + 1) * D]))
        if b < N_BRANCH - 1:
            g01_ref[:, b * D:(b + 1) * D] = gate.astype(BF16)
        else:
            pm_ref[...] = (gate * y_mem).astype(BF16)


def _proj(x, g, lw, tables, kt, vm, *, seq, tm=512):
    T, D = x.shape
    tm = min(tm, seq)
    per_seq = seq // tm
    M = vm.shape[1]
    row = lambda w: pl.BlockSpec((tm, w), lambda i: (i, 0))
    consts = [lw["wa"], lw["wh"], lw["wm"], lw["wg"], lw["qn"], lw["kvn"], lw["wuq"], lw["wuk"],
              lw["wuv"]]
    return pl.pallas_call(
        _proj_kernel,
        out_shape=(jax.ShapeDtypeStruct((T, MLA_HEADS * HEAD_PAD), BF16),
                   jax.ShapeDtypeStruct((T, MLA_HEADS * HEAD_PAD), BF16),
                   jax.ShapeDtypeStruct((T, MLA_HEADS * MLA_V), BF16),
                   jax.ShapeDtypeStruct((T, 4 * HG_W), F32),
                   jax.ShapeDtypeStruct((T, 2 * D), BF16),
                   jax.ShapeDtypeStruct((T, D), BF16)),
        grid=(T // tm,),
        in_specs=[row(D), _const_spec((1, D))] + [_const_spec(c.shape) for c in consts]
                 + [row(LANES), row(LANES), row(LANES),
                    pl.BlockSpec((None, MEM_W, M), lambda i: (i // per_seq, 0, 0)),
                    pl.BlockSpec((None, M, MEM_W), lambda i: (i // per_seq, 0, 0)),
                    _const_spec(lw["wom"].shape)],
        out_specs=(row(MLA_HEADS * HEAD_PAD), row(MLA_HEADS * HEAD_PAD), row(MLA_HEADS * MLA_V),
                   row(4 * HG_W), row(2 * D), row(D)),
        compiler_params=_params(("parallel",)),
        name="mix_proj",
    )(x, g.reshape(1, D), *consts, *tables, kt, vm, lw["wom"])


def _attn_kernel(q_ref, k_ref, v_ref, o_ref, *, tq):
    i = pl.program_id(2)
    row = lax.broadcasted_iota(jnp.int32, (tq, tq), 0)
    col = lax.broadcasted_iota(jnp.int32, (tq, tq), 1)
    lane = lax.broadcasted_iota(jnp.int32, (tq, LANES), 1)
    outs = []
    for hh in range(2):
        hsl = slice(hh * HEAD_PAD, (hh + 1) * HEAD_PAD)
        q = q_ref[:, hsl]

        def step(j, carry, masked):
            m, l, acc = carry
            ks = pl.multiple_of(j * tq, tq)
            s = _dot_nt(q, k_ref[pl.ds(ks, tq), hsl])
            if masked:
                s = jnp.where(row >= col, s, NEG_BIG)
            m_new = jnp.maximum(m, jnp.max(s, axis=-1, keepdims=True))
            alpha = jnp.exp(m - m_new)
            p = jnp.exp(s - m_new)
            l = alpha * l + jnp.sum(p, axis=-1, keepdims=True)
            acc = alpha * acc + _dot(p.astype(BF16), v_ref[pl.ds(ks, tq), :])
            return m_new, l, acc

        init = (jnp.full((tq, 1), NEG_BIG, F32), jnp.zeros((tq, 1), F32),
                jnp.zeros((tq, LANES), F32))
        carry = lax.fori_loop(0, i, functools.partial(step, masked=False), init)
        m, l, acc = step(i, carry, True)
        outs.append(acc / l)
    o_ref[...] = jnp.where(lane < MLA_V, outs[0], outs[1]).astype(BF16)


def _attn(q, k, v, *, batch, seq, tq=512):
    T = q.shape[0]
    tq = min(tq, seq)
    nq = seq // tq
    pairs = MLA_HEADS // 2
    return pl.pallas_call(
        functools.partial(_attn_kernel, tq=tq),
        out_shape=jax.ShapeDtypeStruct((T, MLA_HEADS * MLA_V), BF16),
        grid=(batch, pairs, nq),
        in_specs=[pl.BlockSpec((tq, 2 * HEAD_PAD), lambda b, p, i: (b * nq + i, p)),
                  pl.BlockSpec((seq, 2 * HEAD_PAD), lambda b, p, i: (b, p)),
                  pl.BlockSpec((seq, 2 * MLA_V), lambda b, p, i: (b, p))],
        out_specs=pl.BlockSpec((tq, 2 * MLA_V), lambda b, p, i: (b * nq + i, p)),
        compiler_params=_params(("parallel", "parallel", "arbitrary")),
        name="mla_attn",
    )(q, k, v)


def _level_map(n):
    t = np.arange(n)[:, None]
    s = np.arange(n)[None, :]
    x = np.bitwise_xor(t, s)
    lvl = np.floor(np.log2(np.maximum(x, 1))).astype(np.int32)
    out = np.where(s < t, lvl, -1)
    out = np.where(s == t, int(np.log2(n)), out)
    return out.astype(np.int32)


def _hgrn_kernel(hq_ref, hf_ref, hi_ref, hg_ref, loglb_ref, log1mlb_ref, onemlb_ref, onorm_ref,
                 tri_ref, lvl_ref, o_ref, st_ref, g_scr, *, tc):
    nlev = int(np.log2(tc))

    @pl.when(pl.program_id(1) == 0)
    def _():
        st_ref[...] = jnp.zeros_like(st_ref)

    z = hf_ref[...]
    e = jnp.exp(-jnp.abs(z))
    l1pe = jnp.log1p(e)
    b_ = log1mlb_ref[...] + (jnp.minimum(z, 0.0) - l1pe)
    a_ = loglb_ref[...]
    g = jnp.maximum(a_, b_) + jnp.log1p(jnp.exp(-jnp.abs(a_ - b_)))
    kk = onemlb_ref[...] * jnp.where(z >= 0.0, e, 1.0) / (1.0 + e)
    hq = hq_ref[...]
    qq = hq * _sigmoid(hq)

    g_hi = g.astype(BF16)
    r1 = g - g_hi.astype(F32)
    g_mid = r1.astype(BF16)
    g_lo = (r1 - g_mid.astype(F32)).astype(BF16)
    tri = tri_ref[...]
    G = _dot(tri, g_hi) + _dot(tri, g_mid) + _dot(tri, g_lo)
    g_scr[...] = G
    g_last = g_scr[tc - 1:tc, :]

    rows = lax.broadcasted_iota(jnp.int32, G.shape, 0)
    qe = []
    ke = []
    for lv in range(nlev):
        m = 1 << lv
        upper = (rows & m) != 0
        if lv == 0:
            ed = jnp.where(upper, jnp.exp(g), 1.0)
        else:
            if 2 * m >= SUBLANES:
                parts = [jnp.broadcast_to(g_scr[b * 2 * m + m - 1:b * 2 * m + m, :], (2 * m, G.shape[1]))
                         for b in range(tc // (2 * m))]
            else:
                sub = lax.broadcasted_iota(jnp.int32, (SUBLANES, G.shape[1]), 0)
                parts = []
                for b in range(tc // SUBLANES):
                    r = None
                    for j in reversed(range(SUBLANES // (2 * m))):
                        rowv = jnp.broadcast_to(
                            g_scr[b * SUBLANES + j * 2 * m + m - 1:b * SUBLANES + j * 2 * m + m, :],
                            (SUBLANES, G.shape[1]))
                        r = rowv if r is None else jnp.where(sub < (j + 1) * 2 * m, rowv, r)
                    parts.append(r)
            R = jnp.concatenate(parts, axis=0)
            ed = jnp.exp(-jnp.abs(G - R))
        qe.append(jnp.where(upper, qq * ed, 0.0).astype(BF16))
        ke.append(jnp.where(upper, 0.0, kk * ed).astype(BF16))
    qe.append(qq.astype(BF16))
    ke.append(kk.astype(BF16))

    q_in = (qq * jnp.exp(G)).astype(BF16)
    k_out = (kk * jnp.exp(g_last - G)).astype(BF16)
    d_last = jnp.exp(g_last)
    vv = hi_ref[...].astype(BF16)
    gate = hg_ref[...]
    lvl = lvl_ref[...]
    for h in range(HG_HEADS):
        sl = slice(h * HG_KDIM, (h + 1) * HG_KDIM)
        A = jnp.zeros((tc, tc), F32)
        for lv in range(nlev + 1):
            A = jnp.where(lvl == lv, _dot_nt(qe[lv][:, sl], ke[lv][:, sl]), A)
        st = st_ref[h]
        v = vv[:, sl]
        o = _dot(A.astype(BF16), v) + _dot_nt(q_in[:, sl], st.astype(BF16))
        st_ref[h] = st * d_last[:, sl] + _dot_tn(v, k_out[:, sl])
        on = _rms(o, onorm_ref[...])
        gt = gate[:, sl]
        o_ref[:, sl] = (on * (gt * _sigmoid(gt))).astype(BF16)


def _hgrn(hall, loglb, log1mlb, onemlb, onorm, *, batch, seq, tc=256):
    T = hall.shape[0]
    tc = min(tc, seq)
    nc = seq // tc
    tri = jnp.asarray(np.tril(np.ones((tc, tc), np.float32)), BF16)
    lvl = jnp.asarray(_level_map(tc))
    col = lambda c: pl.BlockSpec((tc, HG_W), lambda b, i: (b * nc + i, c))
    vec = _const_spec((1, HG_W))
    return pl.pallas_call(
        functools.partial(_hgrn_kernel, tc=tc),
        out_shape=jax.ShapeDtypeStruct((T, HG_W), BF16),
        grid=(batch, nc),
        in_specs=[col(0), col(1), col(2), col(3), vec, vec, vec, _const_spec((1, HG_VDIM)),
                  _const_spec((tc, tc)), _const_spec((tc, tc))],
        out_specs=pl.BlockSpec((tc, HG_W), lambda b, i: (b * nc + i, 0)),
        scratch_shapes=[pltpu.VMEM((HG_HEADS, HG_VDIM, HG_KDIM), F32), pltpu.VMEM((tc, HG_W), F32)],
        compiler_params=_params(("parallel", "arbitrary")),
        name="hgrn2",
    )(hall, hall, hall, hall, loglb, log1mlb, onemlb, onorm, tri, lvl)


def _merge_kernel(x_ref, om_ref, oh_ref, g01_ref, pm_ref, womla_ref, wohg_ref, wout_ref, o_ref):
    D = x_ref.shape[1]
    y_mla = _dot(om_ref[...], womla_ref[...])
    y_hg = _dot(oh_ref[...], wohg_ref[...])
    merged = (g01_ref[:, :D].astype(F32) * y_mla + g01_ref[:, D:].astype(F32) * y_hg
              + pm_ref[...].astype(F32))
    o_ref[...] = x_ref[...] + _dot(merged.astype(BF16), wout_ref[...])


def _merge(x, o_mla, o_hg, g01, pm, womla, wohg, wout, *, tm=512):
    T, D = x.shape
    tm = min(tm, T)
    row = lambda w: pl.BlockSpec((tm, w), lambda i: (i, 0))
    return pl.pallas_call(
        _merge_kernel,
        out_shape=jax.ShapeDtypeStruct((T, D), F32),
        grid=(T // tm,),
        in_specs=[row(D), row(o_mla.shape[1]), row(o_hg.shape[1]), row(2 * D), row(D),
                  _const_spec(womla.shape), _const_spec(wohg.shape), _const_spec(wout.shape)],
        out_specs=row(D),
        compiler_params=_params(("parallel",)),
        name="merge",
    )(x, o_mla, o_hg, g01, pm, womla, wohg, wout)


def _prep_weights(w_in, w_uq, w_uk, w_uv, w_mem_kv):
    L, D, _ = w_in.shape
    offs = np.cumsum([0, Q_LORA, KV_LORA, MLA_ROPE, HG_W, HG_W, HG_W, HG_W, MEM_W, N_BRANCH * D])
    kr = w_in[:, :, offs[2]:offs[3]]
    kr_group = jnp.concatenate([jnp.zeros((L, D, MLA_NOPE), F32), kr,
                                jnp.zeros((L, D, HEAD_PAD - MLA_QK), F32)], axis=2)
    wa = jnp.concatenate([w_in[:, :, :offs[2]], kr_group], axis=2)
    wh = w_in[:, :, offs[3]:offs[7]]
    wm = w_in[:, :, offs[7]:offs[8]] * (MEM_HDIM ** -0.5)
    wg = w_in[:, :, offs[8]:offs[9]]
    scale = MLA_QK ** -0.5
    uq = (w_uq * scale).reshape(L, Q_LORA, MLA_HEADS, MLA_QK)
    uq = jnp.pad(uq, ((0, 0), (0, 0), (0, 0), (0, HEAD_PAD - MLA_QK)))
    uk = w_uk.reshape(L, KV_LORA, MLA_HEADS, MLA_NOPE)
    uk = jnp.pad(uk, ((0, 0), (0, 0), (0, 0), (0, HEAD_PAD - MLA_NOPE)))
    wkt = jnp.swapaxes(w_mem_kv[:, :, :MEM_W], 1, 2)
    wv = w_mem_kv[:, :, MEM_W:]
    cast = lambda t: t.astype(BF16)
    return dict(wa=cast(wa), wh=cast(wh), wm=cast(wm), wg=cast(wg),
                wuq=cast(uq.reshape(L, Q_LORA, MLA_HEADS * HEAD_PAD)),
                wuk=cast(uk.reshape(L, KV_LORA, MLA_HEADS * HEAD_PAD)),
                wuv=cast(w_uv), wkt=cast(wkt), wv=cast(wv))


def kernel(x, mem, positions, ffn1_norm, w_ffn1_in, w_ffn1_out, mix_norm, w_in, q_lat_norm, kv_lat_norm,
           w_uq, w_uk, w_uv, w_o_mla, hg_lower_bounds, hg_out_norm, w_o_hg, mem_norm, w_mem_kv, w_o_mem,
           w_out, ffn2_norm, w_ffn2_in, w_ffn2_out, final_norm):
    B, S, D = x.shape
    L = w_in.shape[0]
    T = B * S
    pw = _prep_weights(w_in, w_uq, w_uk, w_uv, w_mem_kv)
    w1a, w1b = w_ffn1_in.astype(BF16), w_ffn1_out.astype(BF16)
    w2a, w2b = w_ffn2_in.astype(BF16), w_ffn2_out.astype(BF16)
    womla, wohg, wom, wout = (w_o_mla.astype(BF16), w_o_hg.astype(BF16), w_o_mem.astype(BF16),
                              w_out.astype(BF16))

    tables = _rope_tables(positions)
    loglb, log1mlb, onemlb = _lower_bounds(hg_lower_bounds)
    kt_all, vm_all = _memkv(mem, mem_norm, pw["wkt"], pw["wv"])

    xf = x.reshape(T, D)
    for l in range(L):
        xf = _ffn(xf, ffn1_norm[l], w1a[l], w1b[l], final_norm, final_norm=False)
        lw = dict(wa=pw["wa"][l], wh=pw["wh"][l], wm=pw["wm"][l], wg=pw["wg"][l],
                  qn=q_lat_norm[l].reshape(1, -1), kvn=kv_lat_norm[l].reshape(1, -1),
                  wuq=pw["wuq"][l], wuk=pw["wuk"][l], wuv=pw["wuv"][l], wom=wom[l])
        q, k, v, hall, g01, pm = _proj(xf, mix_norm[l], lw, tables, kt_all[l], vm_all[l], seq=S)
        o_mla = _attn(q, k, v, batch=B, seq=S)
        o_hg = _hgrn(hall, loglb[l:l + 1], log1mlb[l:l + 1], onemlb[l:l + 1],
                     hg_out_norm[l].reshape(1, -1), batch=B, seq=S)
        xf = _merge(xf, o_mla, o_hg, g01, pm, womla[l], wohg[l], wout[l])
        xf = _ffn(xf, ffn2_norm[l], w2a[l], w2b[l], final_norm, final_norm=(l == L - 1))
    return xf.reshape(B, S, D)
```

```python
import functools
import math

import numpy as np
import jax
import jax.numpy as jnp
from jax import lax
from jax.experimental import pallas as pl
from jax.experimental.pallas import tpu as pltpu

F32 = jnp.float32
BF16 = jnp.bfloat16

MLA_HEADS = 8
MLA_NOPE = 64
MLA_ROPE = 32
MLA_V = 64
Q_LORA = 384
KV_LORA = 256
ROPE_THETA = 10000.0
HG_HEADS = 4
HG_KDIM = 128
HG_VDIM = 128
MEM_HEADS = 4
MEM_HDIM = 128
N_BRANCH = 3
NORM_EPS = 1e-6
MLA_QK = MLA_NOPE + MLA_ROPE
HG_W = HG_HEADS * HG_KDIM
MEM_W = MEM_HEADS * MEM_HDIM

LANES = 128
SUBLANES = 8
VMEM_LIMIT_BYTES = 56 * 1024 * 1024

HEAD_PAD = LANES
ROPE_HALF = MLA_ROPE // 2
NEG_BIG = -1e30
LOG2E = math.log2(math.e)


def _rms(x, g):
    ms = jnp.mean(x * x, axis=-1, keepdims=True)
    return x * lax.rsqrt(ms + NORM_EPS) * g


def _sigmoid(x):
    return 1.0 / (1.0 + jnp.exp(-x))


def _dot(a, b):
    return jnp.dot(a, b, preferred_element_type=F32)


def _dot_nt(a, b):
    return lax.dot_general(a, b, (((1,), (1,)), ((), ())), preferred_element_type=F32)


def _dot_tn(a, b):
    return lax.dot_general(a, b, (((0,), (0,)), ((), ())), preferred_element_type=F32)


def _const_spec(shape):
    nd = len(shape)
    return pl.BlockSpec(shape, lambda *_: (0,) * nd, pipeline_mode=pl.Buffered(1))


def _layer_spec(arr, layer):
    nd = arr.ndim - 1
    return pl.BlockSpec((None,) + arr.shape[1:], lambda *_: (layer,) + (0,) * nd,
                        pipeline_mode=pl.Buffered(1))


def _params(sem):
    return pltpu.CompilerParams(dimension_semantics=sem, vmem_limit_bytes=VMEM_LIMIT_BYTES)


def _rope_row_kernel(pos_ref, freq_ref, cf_ref, s1_ref, s2_ref):
    ang = pos_ref[...].astype(F32) * freq_ref[...]
    c = jnp.cos(ang)
    s = jnp.sin(ang)
    lane = lax.broadcasted_iota(jnp.int32, ang.shape, 1)
    lo = (lane >= MLA_NOPE) & (lane < MLA_NOPE + ROPE_HALF)
    hi = (lane >= MLA_NOPE + ROPE_HALF) & (lane < MLA_QK)
    cf_ref[...] = jnp.where(lane < MLA_NOPE, 1.0, jnp.where(lane < MLA_QK, c, 0.0))
    s1_ref[...] = jnp.where(lo, -s, 0.0)
    s2_ref[...] = jnp.where(hi, s, 0.0)


def _rope_col_kernel(pos_ref, freq_ref, cos_ref, sin_ref):
    ang = freq_ref[...] * pos_ref[...].astype(F32)
    cos_ref[...] = jnp.cos(ang)
    sin_ref[...] = jnp.sin(ang)


def _rope_tables(positions):
    T = positions.size
    tm = min(T, 1024)
    inv = ROPE_THETA ** (-jnp.arange(0, MLA_ROPE, 2, dtype=F32) / MLA_ROPE)
    freq = jnp.concatenate([jnp.zeros((MLA_NOPE,), F32), inv, inv,
                            jnp.zeros((LANES - MLA_QK,), F32)]).reshape(1, LANES)
    out = jax.ShapeDtypeStruct((T, LANES), F32)
    row = pl.BlockSpec((tm, LANES), lambda i: (i, 0))
    rows = pl.pallas_call(
        _rope_row_kernel,
        out_shape=(out, out, out),
        grid=(T // tm,),
        in_specs=[pl.BlockSpec((tm, 1), lambda i: (i, 0)), _const_spec((1, LANES))],
        out_specs=(row, row, row),
        compiler_params=_params(("parallel",)),
        name="rope_rows",
    )(positions.reshape(T, 1), freq)
    outc = jax.ShapeDtypeStruct((ROPE_HALF, T), F32)
    col = pl.BlockSpec((ROPE_HALF, tm), lambda i: (0, i))
    cols = pl.pallas_call(
        _rope_col_kernel,
        out_shape=(outc, outc),
        grid=(T // tm,),
        in_specs=[pl.BlockSpec((1, tm), lambda i: (0, i)), _const_spec((ROPE_HALF, 1))],
        out_specs=(col, col),
        compiler_params=_params(("parallel",)),
        name="rope_cols",
    )(positions.reshape(1, T), inv.reshape(ROPE_HALF, 1))
    return rows, cols


def _lower_bound_kernel(x_ref, lb_ref, onemlb_ref):
    x = x_ref[...]
    depth = x.shape[0]
    e = jnp.exp(x - jnp.max(x, axis=0, keepdims=True))
    p = e / jnp.sum(e, axis=0, keepdims=True)
    acc = jnp.zeros_like(p[0:1])
    for l in range(depth):
        if l > 0:
            acc = acc + p[l:l + 1]
        lb_ref[l:l + 1, :] = acc
        onemlb_ref[l:l + 1, :] = 1.0 - acc


def _lower_bounds(hg_lower_bounds):
    out = jax.ShapeDtypeStruct(hg_lower_bounds.shape, F32)
    return pl.pallas_call(_lower_bound_kernel, out_shape=(out, out),
                          name="hgrn_lower_bounds")(hg_lower_bounds.astype(F32))


def _memkv_kernel(mem_ref, g_ref, wkt_ref, wv_ref, kt_ref, v_ref):
    mn = _rms(mem_ref[...], g_ref[...]).astype(BF16)
    kt_ref[...] = _dot_nt(wkt_ref[...], mn).astype(BF16)
    v_ref[...] = _dot(mn, wv_ref[...]).astype(BF16)


def _memkv(mem, mem_norm, wkt, wv):
    B, M, D = mem.shape
    L = mem_norm.shape[0]
    return pl.pallas_call(
        _memkv_kernel,
        out_shape=(jax.ShapeDtypeStruct((L, B, MEM_W, M), BF16),
                   jax.ShapeDtypeStruct((L, B, M, MEM_W), BF16)),
        grid=(L, B),
        in_specs=[pl.BlockSpec((None, M, D), lambda l, b: (b, 0, 0)),
                  pl.BlockSpec((None, 1, D), lambda l, b: (l, 0, 0)),
                  pl.BlockSpec((None, MEM_W, D), lambda l, b: (l, 0, 0)),
                  pl.BlockSpec((None, D, MEM_W), lambda l, b: (l, 0, 0))],
        out_specs=(pl.BlockSpec((None, None, MEM_W, M), lambda l, b: (l, b, 0, 0)),
                   pl.BlockSpec((None, None, M, MEM_W), lambda l, b: (l, b, 0, 0))),
        compiler_params=_params(("parallel", "parallel")),
        name="mem_kv",
    )(mem, mem_norm.reshape(L, 1, D), wkt, wv)


def _ffn_kernel(x_ref, g_ref, w1_ref, w2_ref, gf_ref, o_ref, *, d_ff, tf, final_norm):
    x = x_ref[...]
    xn = _rms(x, g_ref[...]).astype(BF16)
    acc = None
    for c in range(d_ff // tf):
        a = _dot(xn, w1_ref[:, c * tf:(c + 1) * tf])
        b = _dot(xn, w1_ref[:, d_ff + c * tf:d_ff + (c + 1) * tf])
        h = (a * _sigmoid(a) * b).astype(BF16)
        part = _dot(h, w2_ref[c * tf:(c + 1) * tf, :])
        acc = part if acc is None else acc + part
    y = x + 0.5 * acc
    if final_norm:
        y = _rms(y, gf_ref[...])
    o_ref[...] = y


def _ffn(x, g, w1, w2, gf, layer, *, final_norm, tm=512, tf=1408):
    T, D = x.shape
    d_ff = w2.shape[1]
    tm = min(tm, T)
    row = pl.BlockSpec((tm, D), lambda i: (i, 0))
    return pl.pallas_call(
        functools.partial(_ffn_kernel, d_ff=d_ff, tf=tf, final_norm=final_norm),
        out_shape=jax.ShapeDtypeStruct((T, D), F32),
        grid=(T // tm,),
        in_specs=[row, _layer_spec(g, layer), _layer_spec(w1, layer), _layer_spec(w2, layer),
                  _const_spec((1, D))],
        out_specs=row,
        compiler_params=_params(("parallel",)),
        name="ffn",
    )(x, g, w1, w2, gf.reshape(1, D))


def _proj_kernel(x_ref, g_ref, wa_ref, wh_ref, wm_ref, wg_ref, qn_ref, kvn_ref, wuqt_ref, wuk_ref,
                 wuvt_ref, wom_ref, cf_ref, s1_ref, s2_ref, cos_ref, sin_ref, kt_ref, vm_ref,
                 qt_ref, k_ref, vt_ref, h_ref, g01_ref, pm_ref):
    D = x_ref.shape[1]
    u = _rms(x_ref[...], g_ref[...]).astype(BF16)

    za = _dot(u, wa_ref[...])
    cq = _rms(za[:, :Q_LORA], qn_ref[...]).astype(BF16)
    ckv = _rms(za[:, Q_LORA:Q_LORA + KV_LORA], kvn_ref[...]).astype(BF16)
    t = za[:, Q_LORA + KV_LORA:]
    kr = (t * cf_ref[...] + pltpu.roll(t, LANES - ROPE_HALF, axis=1) * s1_ref[...]
          + pltpu.roll(t, ROPE_HALF, axis=1) * s2_ref[...])
    k = _dot(ckv, wuk_ref[...])
    for h in range(MLA_HEADS):
        sl = slice(h * HEAD_PAD, (h + 1) * HEAD_PAD)
        k_ref[:, sl] = (k[:, sl] + kr).astype(BF16)

    qt = _dot_nt(wuqt_ref[...], cq)
    cos = cos_ref[...]
    sin = sin_ref[...]
    for h in range(MLA_HEADS):
        r0 = h * HEAD_PAD
        t1 = qt[r0 + MLA_NOPE:r0 + MLA_NOPE + ROPE_HALF]
        t2 = qt[r0 + MLA_NOPE + ROPE_HALF:r0 + MLA_QK]
        qt_ref[r0:r0 + MLA_NOPE, :] = qt[r0:r0 + MLA_NOPE].astype(BF16)
        qt_ref[r0 + MLA_NOPE:r0 + MLA_NOPE + ROPE_HALF, :] = (t1 * cos - t2 * sin).astype(BF16)
        qt_ref[r0 + MLA_NOPE + ROPE_HALF:r0 + MLA_QK, :] = (t2 * cos + t1 * sin).astype(BF16)
        qt_ref[r0 + MLA_QK:r0 + HEAD_PAD, :] = jnp.zeros((HEAD_PAD - MLA_QK, qt.shape[1]), BF16)
    vt = _dot_nt(wuvt_ref[...], ckv)
    rowi = lax.broadcasted_iota(jnp.int32, vt.shape, 0)
    vt_ref[...] = jnp.where((rowi & (HEAD_PAD - 1)) == MLA_V, 1.0, vt).astype(BF16)

    for c in range(4):
        sl = slice(c * HG_W, (c + 1) * HG_W)
        h_ref[:, sl] = _dot(u, wh_ref[:, sl])

    mq = _dot(u, wm_ref[...]).astype(BF16)
    heads = []
    for h in range(MEM_HEADS):
        sl = slice(h * MEM_HDIM, (h + 1) * MEM_HDIM)
        s = _dot(mq[:, sl], kt_ref[sl, :])
        p = jnp.exp(s - jnp.max(s, axis=-1, keepdims=True))
        o = _dot(p.astype(BF16), vm_ref[:, sl])
        heads.append((o / jnp.sum(p, axis=-1, keepdims=True)).astype(BF16))
    y_mem = _dot(jnp.concatenate(heads, axis=1), wom_ref[...])

    for b in range(N_BRANCH):
        gate = _sigmoid(_dot(u, wg_ref[:, b * D:(b + 1) * D]))
        if b < N_BRANCH - 1:
            g01_ref[:, b * D:(b + 1) * D] = gate.astype(BF16)
        else:
            pm_ref[...] = (gate * y_mem).astype(BF16)


def _proj(x, stacked, layer, rows, cols, kt, vm, *, seq, tm=512):
    T, D = x.shape
    tm = min(tm, seq)
    per_seq = seq // tm
    M = vm.shape[2]
    G = MLA_HEADS * HEAD_PAD
    row = lambda w: pl.BlockSpec((tm, w), lambda i: (i, 0))
    col = lambda h: pl.BlockSpec((h, tm), lambda i: (0, i))
    return pl.pallas_call(
        _proj_kernel,
        out_shape=(jax.ShapeDtypeStruct((G, T), BF16),
                   jax.ShapeDtypeStruct((T, G), BF16),
                   jax.ShapeDtypeStruct((G, T), BF16),
                   jax.ShapeDtypeStruct((T, 4 * HG_W), F32),
                   jax.ShapeDtypeStruct((T, 2 * D), BF16),
                   jax.ShapeDtypeStruct((T, D), BF16)),
        grid=(T // tm,),
        in_specs=[row(D)] + [_layer_spec(w, layer) for w in stacked]
                 + [row(LANES), row(LANES), row(LANES), col(ROPE_HALF), col(ROPE_HALF),
                    pl.BlockSpec((None, None, MEM_W, M), lambda i: (layer, i // per_seq, 0, 0)),
                    pl.BlockSpec((None, None, M, MEM_W), lambda i: (layer, i // per_seq, 0, 0))],
        out_specs=(col(G), row(G), col(G), row(4 * HG_W), row(2 * D), row(D)),
        compiler_params=_params(("parallel",)),
        name="mix_proj",
    )(x, *stacked, *rows, *cols, kt, vm)


def _attn_kernel(qt_ref, k_ref, vt_ref, o_ref, sa0, sa1, sb0, sb1, *, tq):
    i = pl.program_id(2)
    key = lax.broadcasted_iota(jnp.int32, (tq, tq), 0)
    qry = lax.broadcasted_iota(jnp.int32, (tq, tq), 1)
    heads = [slice(hh * HEAD_PAD, (hh + 1) * HEAD_PAD) for hh in range(2)]
    buf_a = (sa0, sa1)
    buf_b = (sb0, sb1)

    def scores(j, hh, dst):
        ks = pl.multiple_of(j * tq, tq)
        dst[hh][...] = _dot(k_ref[pl.ds(ks, tq), heads[hh]], qt_ref[heads[hh], :])

    def consume(j, hh, src, state, masked):
        m, acc = state
        ks = pl.multiple_of(j * tq, tq)
        s = src[hh][...]
        if masked:
            s = jnp.where(key <= qry, s, NEG_BIG)
        m_new = jnp.maximum(m, jnp.max(s, axis=0, keepdims=True))
        p = jnp.exp2(s - m_new).astype(BF16)
        acc = jnp.exp2(m - m_new) * acc + _dot(vt_ref[heads[hh], pl.ds(ks, tq)], p)
        return m_new, acc

    def step(j, src, dst, carry, masked=False, prefetch=True):
        out = []
        for hh in range(2):
            if prefetch:
                scores(j + 1, hh, dst)
            out.append(consume(j, hh, src, carry[hh], masked))
        return tuple(out)

    def finish(carry):
        for hh, (_, acc) in enumerate(carry):
            o_ref[hh * MLA_V:(hh + 1) * MLA_V, :] = (acc[:MLA_V] / acc[MLA_V:MLA_V + 1]).astype(BF16)

    for hh in range(2):
        scores(0, hh, buf_a)

    def two_steps(t, carry):
        carry = step(2 * t, buf_a, buf_b, carry)
        return step(2 * t + 1, buf_b, buf_a, carry)

    init = tuple((jnp.full((1, tq), NEG_BIG, F32), jnp.zeros((HEAD_PAD, tq), F32)) for _ in heads)
    carry = lax.fori_loop(0, i // 2, two_steps, init)

    @pl.when(i % 2 == 0)
    def _():
        finish(step(i, buf_a, buf_b, carry, masked=True, prefetch=False))

    @pl.when(i % 2 == 1)
    def _():
        c = step(i - 1, buf_a, buf_b, carry)
        finish(step(i, buf_b, buf_a, c, masked=True, prefetch=False))


def _attn(qt, k, vt, *, batch, seq, tq=512):
    T = k.shape[0]
    tq = min(tq, seq)
    nq = seq // tq
    pairs = MLA_HEADS // 2
    return pl.pallas_call(
        functools.partial(_attn_kernel, tq=tq),
        out_shape=jax.ShapeDtypeStruct((MLA_HEADS * MLA_V, T), BF16),
        grid=(batch, pairs, nq),
        in_specs=[pl.BlockSpec((2 * HEAD_PAD, tq), lambda b, p, i: (p, b * nq + i)),
                  pl.BlockSpec((seq, 2 * HEAD_PAD), lambda b, p, i: (b, p)),
                  pl.BlockSpec((2 * HEAD_PAD, seq), lambda b, p, i: (p, b))],
        out_specs=pl.BlockSpec((2 * MLA_V, tq), lambda b, p, i: (p, b * nq + i)),
        scratch_shapes=[pltpu.VMEM((tq, tq), F32)] * 4,
        compiler_params=_params(("parallel", "parallel", "arbitrary")),
        name="mla_attn",
    )(qt, k, vt)


def _level_map(n):
    t = np.arange(n)[:, None]
    s = np.arange(n)[None, :]
    x = np.bitwise_xor(t, s)
    lvl = np.floor(np.log2(np.maximum(x, 1))).astype(np.int32)
    out = np.where(s < t, lvl, -1)
    out = np.where(s == t, int(np.log2(n)), out)
    return out.astype(np.int32)


def _hgrn_kernel(hq_ref, hf_ref, hi_ref, hg_ref, lb_ref, onemlb_ref, onorm_ref, tri_ref, lvl_ref,
                 o_ref, st_ref, g_scr, *, tc):
    half = tc // 2
    nlev = int(np.log2(half))
    W = hf_ref.shape[1]

    @pl.when(pl.program_id(1) == 0)
    def _():
        st_ref[...] = jnp.zeros_like(st_ref)

    z = hf_ref[...]
    lb = lb_ref[...]
    e = jnp.exp(-jnp.abs(z))
    pos = z >= 0.0
    lden = jnp.log(1.0 + e)
    g = jnp.log(jnp.where(pos, 1.0 + lb * e, lb + e)) - lden
    g = jnp.maximum(g, jnp.minimum(z, 0.0) - lden) * LOG2E
    kk = onemlb_ref[...] * jnp.where(pos, e, 1.0) / (1.0 + e)
    hq = hq_ref[...]
    qq = hq * _sigmoid(hq)

    g_hi = g.astype(BF16)
    r1 = g - g_hi.astype(F32)
    g_mid = r1.astype(BF16)
    g_lo = (r1 - g_mid.astype(F32)).astype(BF16)
    tri = tri_ref[...]
    G = _dot(tri, g_hi) + _dot(tri, g_mid) + _dot(tri, g_lo)
    g_scr[...] = G
    g_last = g_scr[tc - 1:tc, :]

    qb = qq.astype(BF16)
    kb = kk.astype(BF16)
    rows = lax.broadcasted_iota(jnp.int32, G.shape, 0)
    qe = []
    ke = []
    for lv in range(nlev + 1):
        m = 1 << lv
        if lv == 0:
            d = jnp.where((rows & 1) != 0, g, 0.0)
        else:
            if 2 * m >= SUBLANES:
                parts = [jnp.broadcast_to(g_scr[b * 2 * m + m - 1:b * 2 * m + m, :], (2 * m, W))
                         for b in range(tc // (2 * m))]
            else:
                sub = lax.broadcasted_iota(jnp.int32, (SUBLANES, W), 0)
                parts = []
                for b in range(tc // SUBLANES):
                    r = None
                    for j in reversed(range(SUBLANES // (2 * m))):
                        r0 = b * SUBLANES + j * 2 * m + m - 1
                        rowv = jnp.broadcast_to(g_scr[r0:r0 + 1, :], (SUBLANES, W))
                        r = rowv if r is None else jnp.where(sub < (j + 1) * 2 * m, rowv, r)
                    parts.append(r)
            d = -jnp.abs(G - jnp.concatenate(parts, axis=0))
        ed = jnp.exp2(d.astype(BF16))
        qe.append(qb * ed)
        ke.append(kb * ed)

    q_in = (qq * jnp.exp2(G)).astype(BF16)
    k_out = (kk * jnp.exp2(g_last - G)).astype(BF16)
    d_last = jnp.exp2(g_last)
    vv = hi_ref[...].astype(BF16)
    gate = hg_ref[...]
    lvl = lvl_ref[...]
    for h in range(HG_HEADS):
        sl = slice(h * HG_KDIM, (h + 1) * HG_KDIM)
        diag = []
        for r0 in (0, half):
            rs = slice(r0, r0 + half)
            A = jnp.where(lvl == nlev, _dot_nt(qb[rs, sl], kb[rs, sl]), 0.0)
            for lv in range(nlev):
                A = jnp.where(lvl == lv, _dot_nt(qe[lv][rs, sl], ke[lv][rs, sl]), A)
            diag.append(A.astype(BF16))
        cross = _dot_nt(qe[nlev][half:, sl], ke[nlev][:half, sl]).astype(BF16)
        st = st_ref[h]
        v = vv[:, sl]
        o_int = _dot_nt(q_in[:, sl], st.astype(BF16))
        o_top = _dot(diag[0], v[:half]) + o_int[:half]
        o_bot = _dot(cross, v[:half]) + _dot(diag[1], v[half:]) + o_int[half:]
        st_ref[h] = st * d_last[:, sl] + _dot_tn(v, k_out[:, sl])
        gt = gate[:, sl]
        sg = gt * _sigmoid(gt)
        o_ref[:half, sl] = (_rms(o_top, onorm_ref[...]) * sg[:half]).astype(BF16)
        o_ref[half:, sl] = (_rms(o_bot, onorm_ref[...]) * sg[half:]).astype(BF16)


def _hgrn(hall, lb, onemlb, onorm, layer, *, batch, seq, tc=256):
    T = hall.shape[0]
    tc = min(tc, seq)
    nc = seq // tc
    tri = jnp.asarray(np.tril(np.ones((tc, tc), np.float32)), BF16)
    lvl = jnp.asarray(_level_map(tc // 2))
    col = lambda c: pl.BlockSpec((tc, HG_W), lambda b, i: (b * nc + i, c))
    return pl.pallas_call(
        functools.partial(_hgrn_kernel, tc=tc),
        out_shape=jax.ShapeDtypeStruct((T, HG_W), BF16),
        grid=(batch, nc),
        in_specs=[col(0), col(1), col(2), col(3), _layer_spec(lb, layer), _layer_spec(onemlb, layer),
                  _layer_spec(onorm, layer), _const_spec((tc, tc)), _const_spec((tc // 2, tc // 2))],
        out_specs=pl.BlockSpec((tc, HG_W), lambda b, i: (b * nc + i, 0)),
        scratch_shapes=[pltpu.VMEM((HG_HEADS, HG_VDIM, HG_KDIM), F32), pltpu.VMEM((tc, HG_W), F32)],
        compiler_params=_params(("parallel", "arbitrary")),
        name="hgrn2",
    )(hall, hall, hall, hall, lb, onemlb, onorm, tri, lvl)


def _merge_kernel(x_ref, omt_ref, oh_ref, g01_ref, pm_ref, womla_ref, wohg_ref, wout_ref, o_ref):
    D = x_ref.shape[1]
    y_mla = _dot_tn(omt_ref[...], womla_ref[...])
    y_hg = _dot(oh_ref[...], wohg_ref[...])
    merged = (g01_ref[:, :D].astype(F32) * y_mla + g01_ref[:, D:].astype(F32) * y_hg
              + pm_ref[...].astype(F32))
    o_ref[...] = x_ref[...] + _dot(merged.astype(BF16), wout_ref[...])


def _merge(x, o_mla_t, o_hg, g01, pm, womla, wohg, wout, layer, *, tm=512):
    T, D = x.shape
    tm = min(tm, T)
    row = lambda w: pl.BlockSpec((tm, w), lambda i: (i, 0))
    return pl.pallas_call(
        _merge_kernel,
        out_shape=jax.ShapeDtypeStruct((T, D), F32),
        grid=(T // tm,),
        in_specs=[row(D), pl.BlockSpec((o_mla_t.shape[0], tm), lambda i: (0, i)), row(o_hg.shape[1]),
                  row(2 * D), row(D), _layer_spec(womla, layer), _layer_spec(wohg, layer),
                  _layer_spec(wout, layer)],
        out_specs=row(D),
        compiler_params=_params(("parallel",)),
        name="merge",
    )(x, o_mla_t, o_hg, g01, pm, womla, wohg, wout)


def _prep_weights(w_in, w_uq, w_uk, w_uv, w_mem_kv):
    L, D, _ = w_in.shape
    offs = np.cumsum([0, Q_LORA, KV_LORA, MLA_ROPE, HG_W, HG_W, HG_W, HG_W, MEM_W, N_BRANCH * D])
    kr = w_in[:, :, offs[2]:offs[3]]
    kr_group = jnp.concatenate([jnp.zeros((L, D, MLA_NOPE), F32), kr,
                                jnp.zeros((L, D, HEAD_PAD - MLA_QK), F32)], axis=2)
    wa = jnp.concatenate([w_in[:, :, :offs[2]], kr_group], axis=2)
    wh = w_in[:, :, offs[3]:offs[7]]
    wm = w_in[:, :, offs[7]:offs[8]] * (MEM_HDIM ** -0.5)
    wg = w_in[:, :, offs[8]:offs[9]]
    uq = (w_uq * (MLA_QK ** -0.5 * LOG2E)).reshape(L, Q_LORA, MLA_HEADS, MLA_QK)
    uq = jnp.pad(uq, ((0, 0), (0, 0), (0, 0), (0, HEAD_PAD - MLA_QK)))
    uk = w_uk.reshape(L, KV_LORA, MLA_HEADS, MLA_NOPE)
    uk = jnp.pad(uk, ((0, 0), (0, 0), (0, 0), (0, HEAD_PAD - MLA_NOPE)))
    uv = w_uv.reshape(L, KV_LORA, MLA_HEADS, MLA_V)
    uv = jnp.pad(uv, ((0, 0), (0, 0), (0, 0), (0, HEAD_PAD - MLA_V)))
    G = MLA_HEADS * HEAD_PAD
    wkt = jnp.swapaxes(w_mem_kv[:, :, :MEM_W], 1, 2)
    wv = w_mem_kv[:, :, MEM_W:]
    cast = lambda t: t.astype(BF16)
    return dict(wa=cast(wa), wh=cast(wh), wm=cast(wm), wg=cast(wg),
                wuqt=cast(jnp.swapaxes(uq.reshape(L, Q_LORA, G), 1, 2)),
                wuk=cast(uk.reshape(L, KV_LORA, G)),
                wuvt=cast(jnp.swapaxes(uv.reshape(L, KV_LORA, G), 1, 2)),
                wkt=cast(wkt), wv=cast(wv))


def kernel(x, mem, positions, ffn1_norm, w_ffn1_in, w_ffn1_out, mix_norm, w_in, q_lat_norm, kv_lat_norm,
           w_uq, w_uk, w_uv, w_o_mla, hg_lower_bounds, hg_out_norm, w_o_hg, mem_norm, w_mem_kv, w_o_mem,
           w_out, ffn2_norm, w_ffn2_in, w_ffn2_out, final_norm):
    B, S, D = x.shape
    L = w_in.shape[0]
    T = B * S
    vec = lambda t: t.reshape(L, 1, -1)
    pw = _prep_weights(w_in, w_uq, w_uk, w_uv, w_mem_kv)
    w1a, w1b = w_ffn1_in.astype(BF16), w_ffn1_out.astype(BF16)
    w2a, w2b = w_ffn2_in.astype(BF16), w_ffn2_out.astype(BF16)
    womla, wohg, wout = w_o_mla.astype(BF16), w_o_hg.astype(BF16), w_out.astype(BF16)
    stacked = [vec(mix_norm), pw["wa"], pw["wh"], pw["wm"], pw["wg"], vec(q_lat_norm), vec(kv_lat_norm),
               pw["wuqt"], pw["wuk"], pw["wuvt"], w_o_mem.astype(BF16)]

    rows, cols = _rope_tables(positions)
    lb, onemlb = _lower_bounds(hg_lower_bounds)
    kt_all, vm_all = _memkv(mem, mem_norm, pw["wkt"], pw["wv"])

    xf = x.reshape(T, D)
    for l in range(L):
        xf = _ffn(xf, vec(ffn1_norm), w1a, w1b, final_norm, l, final_norm=False)
        qt, k, vt, hall, g01, pm = _proj(xf, stacked, l, rows, cols, kt_all, vm_all, seq=S)
        o_mla_t = _attn(qt, k, vt, batch=B, seq=S)
        o_hg = _hgrn(hall, vec(lb), vec(onemlb), vec(hg_out_norm), l, batch=B, seq=S)
        xf = _merge(xf, o_mla_t, o_hg, g01, pm, womla, wohg, wout, l)
        xf = _ffn(xf, vec(ffn2_norm), w2a, w2b, final_norm, l, final_norm=(l == L - 1))
    return xf.reshape(B, S, D)
```

```python
import functools
import math

import numpy as np
import jax
import jax.numpy as jnp
from jax import lax
from jax.experimental import pallas as pl
from jax.experimental.pallas import tpu as pltpu

F32 = jnp.float32
BF16 = jnp.bfloat16

MLA_HEADS = 8
MLA_NOPE = 64
MLA_ROPE = 32
MLA_V = 64
Q_LORA = 384
KV_LORA = 256
ROPE_THETA = 10000.0
HG_HEADS = 4
HG_KDIM = 128
HG_VDIM = 128
MEM_HEADS = 4
MEM_HDIM = 128
N_BRANCH = 3
NORM_EPS = 1e-6
MLA_QK = MLA_NOPE + MLA_ROPE
HG_W = HG_HEADS * HG_KDIM
MEM_W = MEM_HEADS * MEM_HDIM

LANES = 128
SUBLANES = 8
MXU_TILE = 256
VMEM_LIMIT_BYTES = 56 * 1024 * 1024

HEAD_PAD = LANES
ROPE_HALF = MLA_ROPE // 2
NEG_BIG = -1e30
LOG2E = math.log2(math.e)


def _rms(x, g):
    ms = jnp.mean(x * x, axis=-1, keepdims=True)
    return x * lax.rsqrt(ms + NORM_EPS) * g


def _sigmoid(x):
    return 1.0 / (1.0 + jnp.exp(-x))


def _dot(a, b):
    return jnp.dot(a, b, preferred_element_type=F32)


def _dot_nt(a, b):
    return lax.dot_general(a, b, (((1,), (1,)), ((), ())), preferred_element_type=F32)


def _dot_tn(a, b):
    return lax.dot_general(a, b, (((0,), (0,)), ((), ())), preferred_element_type=F32)


def _const_spec(shape):
    nd = len(shape)
    return pl.BlockSpec(shape, lambda *_: (0,) * nd, pipeline_mode=pl.Buffered(1))


def _layer_spec(arr, layer):
    nd = arr.ndim - 1
    return pl.BlockSpec((None,) + arr.shape[1:], lambda *_: (layer,) + (0,) * nd,
                        pipeline_mode=pl.Buffered(1))


def _params(sem):
    return pltpu.CompilerParams(dimension_semantics=sem, vmem_limit_bytes=VMEM_LIMIT_BYTES)


def _rope_row_kernel(pos_ref, freq_ref, cf_ref, s1_ref, s2_ref):
    ang = pos_ref[...].astype(F32) * freq_ref[...]
    c = jnp.cos(ang)
    s = jnp.sin(ang)
    lane = lax.broadcasted_iota(jnp.int32, ang.shape, 1)
    lo = (lane >= MLA_NOPE) & (lane < MLA_NOPE + ROPE_HALF)
    hi = (lane >= MLA_NOPE + ROPE_HALF) & (lane < MLA_QK)
    cf_ref[...] = jnp.where(lane < MLA_NOPE, 1.0, jnp.where(lane < MLA_QK, c, 0.0))
    s1_ref[...] = jnp.where(lo, -s, 0.0)
    s2_ref[...] = jnp.where(hi, s, 0.0)


def _rope_col_kernel(pos_ref, freq_ref, cos_ref, sin_ref):
    ang = freq_ref[...] * pos_ref[...].astype(F32)
    cos_ref[...] = jnp.cos(ang)
    sin_ref[...] = jnp.sin(ang)


def _rope_tables(positions):
    T = positions.size
    tm = min(T, 1024)
    inv = ROPE_THETA ** (-jnp.arange(0, MLA_ROPE, 2, dtype=F32) / MLA_ROPE)
    freq = jnp.concatenate([jnp.zeros((MLA_NOPE,), F32), inv, inv,
                            jnp.zeros((LANES - MLA_QK,), F32)]).reshape(1, LANES)
    out = jax.ShapeDtypeStruct((T, LANES), F32)
    row = pl.BlockSpec((tm, LANES), lambda i: (i, 0))
    rows = pl.pallas_call(
        _rope_row_kernel,
        out_shape=(out, out, out),
        grid=(T // tm,),
        in_specs=[pl.BlockSpec((tm, 1), lambda i: (i, 0)), _const_spec((1, LANES))],
        out_specs=(row, row, row),
        compiler_params=_params(("parallel",)),
        name="rope_rows",
    )(positions.reshape(T, 1), freq)
    outc = jax.ShapeDtypeStruct((ROPE_HALF, T), F32)
    col = pl.BlockSpec((ROPE_HALF, tm), lambda i: (0, i))
    cols = pl.pallas_call(
        _rope_col_kernel,
        out_shape=(outc, outc),
        grid=(T // tm,),
        in_specs=[pl.BlockSpec((1, tm), lambda i: (0, i)), _const_spec((ROPE_HALF, 1))],
        out_specs=(col, col),
        compiler_params=_params(("parallel",)),
        name="rope_cols",
    )(positions.reshape(1, T), inv.reshape(ROPE_HALF, 1))
    return rows, cols


def _lower_bound_kernel(x_ref, lb_ref, onemlb_ref):
    x = x_ref[...]
    depth = x.shape[0]
    e = jnp.exp(x - jnp.max(x, axis=0, keepdims=True))
    p = e / jnp.sum(e, axis=0, keepdims=True)
    acc = jnp.zeros_like(p[0:1])
    for l in range(depth):
        if l > 0:
            acc = acc + p[l:l + 1]
        lb_ref[l:l + 1, :] = acc
        onemlb_ref[l:l + 1, :] = 1.0 - acc


def _lower_bounds(hg_lower_bounds):
    out = jax.ShapeDtypeStruct(hg_lower_bounds.shape, F32)
    return pl.pallas_call(_lower_bound_kernel, out_shape=(out, out),
                          name="hgrn_lower_bounds")(hg_lower_bounds.astype(F32))


def _memkv_kernel(mem_ref, g_ref, wkt_ref, wv_ref, kt_ref, v_ref):
    mn = _rms(mem_ref[...], g_ref[...]).astype(BF16)
    kt_ref[...] = _dot_nt(wkt_ref[...], mn).astype(BF16)
    v_ref[...] = _dot(mn, wv_ref[...]).astype(BF16)


def _memkv(mem, mem_norm, wkt, wv):
    B, M, D = mem.shape
    L = mem_norm.shape[0]
    return pl.pallas_call(
        _memkv_kernel,
        out_shape=(jax.ShapeDtypeStruct((L, B, MEM_W, M), BF16),
                   jax.ShapeDtypeStruct((L, B, M, MEM_W), BF16)),
        grid=(L, B),
        in_specs=[pl.BlockSpec((None, M, D), lambda l, b: (b, 0, 0)),
                  pl.BlockSpec((None, 1, D), lambda l, b: (l, 0, 0)),
                  pl.BlockSpec((None, MEM_W, D), lambda l, b: (l, 0, 0)),
                  pl.BlockSpec((None, D, MEM_W), lambda l, b: (l, 0, 0))],
        out_specs=(pl.BlockSpec((None, None, MEM_W, M), lambda l, b: (l, b, 0, 0)),
                   pl.BlockSpec((None, None, M, MEM_W), lambda l, b: (l, b, 0, 0))),
        compiler_params=_params(("parallel", "parallel")),
        name="mem_kv",
    )(mem, mem_norm.reshape(L, 1, D), wkt, wv)


def _ffn_chunks(d_ff, n_chunks):
    tiles = d_ff // MXU_TILE
    assert tiles * MXU_TILE == d_ff
    bounds = [MXU_TILE * ((tiles * c) // n_chunks) for c in range(n_chunks + 1)]
    return list(zip(bounds[:-1], bounds[1:]))


def _ffn_kernel(x_ref, g_ref, w1_ref, w2_ref, gf_ref, o_ref, *, d_ff, n_chunks, final_norm):
    x = x_ref[...]
    xn = _rms(x, g_ref[...]).astype(BF16)
    acc = None
    for lo, hi in _ffn_chunks(d_ff, n_chunks):
        a = _dot(xn, w1_ref[:, lo:hi])
        b = _dot(xn, w1_ref[:, d_ff + lo:d_ff + hi])
        h = (a * _sigmoid(a) * b).astype(BF16)
        part = _dot(h, w2_ref[lo:hi, :])
        acc = part if acc is None else acc + part
    y = x + 0.5 * acc
    if final_norm:
        y = _rms(y, gf_ref[...])
    o_ref[...] = y


def _ffn(x, g, w1, w2, gf, layer, *, final_norm, tm=512, n_chunks=2):
    T, D = x.shape
    d_ff = w2.shape[1]
    tm = min(tm, T)
    row = pl.BlockSpec((tm, D), lambda i: (i, 0))
    return pl.pallas_call(
        functools.partial(_ffn_kernel, d_ff=d_ff, n_chunks=n_chunks, final_norm=final_norm),
        out_shape=jax.ShapeDtypeStruct((T, D), F32),
        grid=(T // tm,),
        in_specs=[row, _layer_spec(g, layer), _layer_spec(w1, layer), _layer_spec(w2, layer),
                  _const_spec((1, D))],
        out_specs=row,
        compiler_params=_params(("parallel",)),
        name="ffn",
    )(x, g, w1, w2, gf.reshape(1, D))


def _level_map(n):
    t = np.arange(n)[:, None]
    s = np.arange(n)[None, :]
    x = np.bitwise_xor(t, s)
    lvl = np.floor(np.log2(np.maximum(x, 1))).astype(np.int32)
    out = np.where(s < t, lvl, -1)
    out = np.where(s == t, int(np.log2(n)), out)
    return out.astype(np.int32)


def _hgrn_stages(h_scr, r0, tc, lb_ref, onemlb_ref, onorm_ref, tri_ref, lvl_ref, o_ref, st_ref, g_scr):
    half = tc // 2
    nlev = int(np.log2(half))
    W = HG_W
    rs_all = slice(r0, r0 + tc)

    z = h_scr[rs_all, HG_W:2 * HG_W]
    lb = lb_ref[...]
    e = jnp.exp(-jnp.abs(z))
    pos = z >= 0.0
    lden = jnp.log(1.0 + e)
    g = jnp.log(jnp.where(pos, 1.0 + lb * e, lb + e)) - lden
    g = jnp.maximum(g, jnp.minimum(z, 0.0) - lden) * LOG2E
    kk = onemlb_ref[...] * jnp.where(pos, e, 1.0) / (1.0 + e)
    hq = h_scr[rs_all, :HG_W]
    qq = hq * _sigmoid(hq)

    g_hi = g.astype(BF16)
    r1 = g - g_hi.astype(F32)
    g_mid = r1.astype(BF16)
    g_lo = (r1 - g_mid.astype(F32)).astype(BF16)
    tri = tri_ref[...]
    G = _dot(tri, g_hi) + _dot(tri, g_mid) + _dot(tri, g_lo)
    g_scr[...] = G
    g_last = g_scr[tc - 1:tc, :]
    qb = qq.astype(BF16)
    kb = kk.astype(BF16)
    yield

    rows = lax.broadcasted_iota(jnp.int32, G.shape, 0)
    qe = []
    ke = []
    for lv in range(nlev + 1):
        m = 1 << lv
        if lv == 0:
            d = jnp.where((rows & 1) != 0, g, 0.0)
        elif m >= SUBLANES:
            parts = []
            for b in range(tc // (2 * m)):
                lo = b * 2 * m
                ref_row = g_scr[lo + m - 1:lo + m, :]
                parts.append(ref_row - G[lo:lo + m])
                parts.append(G[lo + m:lo + 2 * m] - ref_row)
            d = jnp.concatenate(parts, axis=0)
        else:
            sub = lax.broadcasted_iota(jnp.int32, (SUBLANES, W), 0)
            parts = []
            for b in range(tc // SUBLANES):
                r = None
                for j in reversed(range(SUBLANES // (2 * m))):
                    rr = b * SUBLANES + j * 2 * m + m - 1
                    rowv = jnp.broadcast_to(g_scr[rr:rr + 1, :], (SUBLANES, W))
                    r = rowv if r is None else jnp.where(sub < (j + 1) * 2 * m, rowv, r)
                parts.append(r)
            d = -jnp.abs(G - jnp.concatenate(parts, axis=0))
        ed = jnp.exp2(d.astype(BF16))
        qe.append(qb * ed)
        ke.append(kb * ed)
        if lv % 2 == 1:
            yield

    q_in = (qq * jnp.exp2(G)).astype(BF16)
    k_out = (kk * jnp.exp2(g_last - G)).astype(BF16)
    d_last = jnp.exp2(g_last)
    vv = h_scr[rs_all, 2 * HG_W:3 * HG_W].astype(BF16)
    gate = h_scr[rs_all, 3 * HG_W:]
    lvl = lvl_ref[...]
    yield
    for h in range(HG_HEADS):
        sl = slice(h * HG_KDIM, (h + 1) * HG_KDIM)
        diag = []
        for b0 in (0, half):
            rs = slice(b0, b0 + half)
            A = jnp.where(lvl == nlev, _dot_nt(qb[rs, sl], kb[rs, sl]), 0.0)
            for lv in range(nlev):
                A = jnp.where(lvl == lv, _dot_nt(qe[lv][rs, sl], ke[lv][rs, sl]), A)
            diag.append(A.astype(BF16))
        cross = _dot_nt(qe[nlev][half:, sl], ke[nlev][:half, sl]).astype(BF16)
        st = st_ref[h]
        v = vv[:, sl]
        o_int = _dot_nt(q_in[:, sl], st.astype(BF16))
        o_top = _dot(diag[0], v[:half]) + o_int[:half]
        o_bot = _dot(cross, v[:half]) + _dot(diag[1], v[half:]) + o_int[half:]
        st_ref[h] = st * d_last[:, sl] + _dot_tn(v, k_out[:, sl])
        gt = gate[:, sl]
        sg = gt * _sigmoid(gt)
        o_ref[r0:r0 + half, sl] = (_rms(o_top, onorm_ref[...]) * sg[:half]).astype(BF16)
        o_ref[r0 + half:r0 + tc, sl] = (_rms(o_bot, onorm_ref[...]) * sg[half:]).astype(BF16)
        yield


def _proj_stages(u, wa_ref, wm_ref, wg_ref, qn_ref, kvn_ref, wuqt_ref, wuk_ref, wuvt_ref, wom_ref,
                 cf_ref, s1_ref, s2_ref, cos_ref, sin_ref, kt_ref, vm_ref,
                 qt_ref, k_ref, vt_ref, g01_ref, pm_ref):
    D = u.shape[1]
    za = _dot(u, wa_ref[...])
    cq = _rms(za[:, :Q_LORA], qn_ref[...]).astype(BF16)
    ckv = _rms(za[:, Q_LORA:Q_LORA + KV_LORA], kvn_ref[...]).astype(BF16)
    t = za[:, Q_LORA + KV_LORA:]
    kr = (t * cf_ref[...] + pltpu.roll(t, LANES - ROPE_HALF, axis=1) * s1_ref[...]
          + pltpu.roll(t, ROPE_HALF, axis=1) * s2_ref[...])
    k = _dot(ckv, wuk_ref[...])
    for h in range(MLA_HEADS):
        sl = slice(h * HEAD_PAD, (h + 1) * HEAD_PAD)
        k_ref[:, sl] = (k[:, sl] + kr).astype(BF16)
    yield

    qt = _dot_nt(wuqt_ref[...], cq)
    cos = cos_ref[...]
    sin = sin_ref[...]
    for h in range(MLA_HEADS):
        r0 = h * HEAD_PAD
        t1 = qt[r0 + MLA_NOPE:r0 + MLA_NOPE + ROPE_HALF]
        t2 = qt[r0 + MLA_NOPE + ROPE_HALF:r0 + MLA_QK]
        qt_ref[r0:r0 + MLA_NOPE, :] = qt[r0:r0 + MLA_NOPE].astype(BF16)
        qt_ref[r0 + MLA_NOPE:r0 + MLA_NOPE + ROPE_HALF, :] = (t1 * cos - t2 * sin).astype(BF16)
        qt_ref[r0 + MLA_NOPE + ROPE_HALF:r0 + MLA_QK, :] = (t2 * cos + t1 * sin).astype(BF16)
        qt_ref[r0 + MLA_QK:r0 + HEAD_PAD, :] = jnp.zeros((HEAD_PAD - MLA_QK, qt.shape[1]), BF16)
    yield
    vt = _dot_nt(wuvt_ref[...], ckv)
    rowi = lax.broadcasted_iota(jnp.int32, vt.shape, 0)
    vt_ref[...] = jnp.where((rowi & (HEAD_PAD - 1)) == MLA_V, 1.0, vt).astype(BF16)
    yield

    mq = _dot(u, wm_ref[...]).astype(BF16)
    heads = []
    for h in range(MEM_HEADS):
        sl = slice(h * MEM_HDIM, (h + 1) * MEM_HDIM)
        s = _dot(mq[:, sl], kt_ref[sl, :])
        p = jnp.exp(s - jnp.max(s, axis=-1, keepdims=True))
        o = _dot(p.astype(BF16), vm_ref[:, sl])
        heads.append((o / jnp.sum(p, axis=-1, keepdims=True)).astype(BF16))
        if h % 2 == 1:
            yield
    y_mem = _dot(jnp.concatenate(heads, axis=1), wom_ref[...])

    for b in range(N_BRANCH):
        gate = _sigmoid(_dot(u, wg_ref[:, b * D:(b + 1) * D]))
        if b < N_BRANCH - 1:
            g01_ref[:, b * D:(b + 1) * D] = gate.astype(BF16)
        else:
            pm_ref[...] = (gate * y_mem).astype(BF16)
        yield


def _mixer_kernel(x_ref, g_ref, wa_ref, wh_ref, wm_ref, wg_ref, qn_ref, kvn_ref, wuqt_ref, wuk_ref,
                  wuvt_ref, wom_ref, lb_ref, onemlb_ref, onorm_ref, tri_ref, lvl_ref,
                  cf_ref, s1_ref, s2_ref, cos_ref, sin_ref, kt_ref, vm_ref,
                  qt_ref, k_ref, vt_ref, oh_ref, g01_ref, pm_ref, st_ref, g_scr, h_scr, *, tc, per_seq):
    @pl.when(pl.program_id(0) % per_seq == 0)
    def _():
        st_ref[...] = jnp.zeros_like(st_ref)

    u = _rms(x_ref[...], g_ref[...]).astype(BF16)
    for c in range(4):
        sl = slice(c * HG_W, (c + 1) * HG_W)
        h_scr[:, sl] = _dot(u, wh_ref[:, sl])

    proj = _proj_stages(u, wa_ref, wm_ref, wg_ref, qn_ref, kvn_ref, wuqt_ref, wuk_ref, wuvt_ref, wom_ref,
                        cf_ref, s1_ref, s2_ref, cos_ref, sin_ref, kt_ref, vm_ref,
                        qt_ref, k_ref, vt_ref, g01_ref, pm_ref)
    tm = x_ref.shape[0]
    recur = (stage for r0 in range(0, tm, tc)
             for stage in _hgrn_stages(h_scr, r0, tc, lb_ref, onemlb_ref, onorm_ref, tri_ref, lvl_ref,
                                       oh_ref, st_ref, g_scr))
    live = [recur, recur, proj]
    while live:
        for gen in list(live):
            if gen in live and next(gen, StopIteration) is StopIteration:
                live = [g for g in live if g is not gen]


def _mixer(x, stacked, layer, rows, cols, kt, vm, *, seq, tm=512, tc=256):
    T, D = x.shape
    tm = min(tm, seq)
    tc = min(tc, tm)
    per_seq = seq // tm
    M = vm.shape[2]
    G = MLA_HEADS * HEAD_PAD
    tri = jnp.asarray(np.tril(np.ones((tc, tc), np.float32)), BF16)
    lvl = jnp.asarray(_level_map(tc // 2))
    row = lambda w: pl.BlockSpec((tm, w), lambda i: (i, 0))
    col = lambda h: pl.BlockSpec((h, tm), lambda i: (0, i))
    return pl.pallas_call(
        functools.partial(_mixer_kernel, tc=tc, per_seq=per_seq),
        out_shape=(jax.ShapeDtypeStruct((G, T), BF16),
                   jax.ShapeDtypeStruct((T, G), BF16),
                   jax.ShapeDtypeStruct((G, T), BF16),
                   jax.ShapeDtypeStruct((T, HG_W), BF16),
                   jax.ShapeDtypeStruct((T, 2 * D), BF16),
                   jax.ShapeDtypeStruct((T, D), BF16)),
        grid=(T // tm,),
        in_specs=[row(D)] + [_layer_spec(w, layer) for w in stacked]
                 + [_const_spec((tc, tc)), _const_spec((tc // 2, tc // 2)),
                    row(LANES), row(LANES), row(LANES), col(ROPE_HALF), col(ROPE_HALF),
                    pl.BlockSpec((None, None, MEM_W, M), lambda i: (layer, i // per_seq, 0, 0)),
                    pl.BlockSpec((None, None, M, MEM_W), lambda i: (layer, i // per_seq, 0, 0))],
        out_specs=(col(G), row(G), col(G), row(HG_W), row(2 * D), row(D)),
        scratch_shapes=[pltpu.VMEM((HG_HEADS, HG_VDIM, HG_KDIM), F32), pltpu.VMEM((tc, HG_W), F32),
                        pltpu.VMEM((tm, 4 * HG_W), F32)],
        compiler_params=_params(("arbitrary",)),
        name="mixer_proj_hgrn",
    )(x, *stacked, tri, lvl, *rows, *cols, kt, vm)


def _attn_kernel(qt_ref, k_ref, vt_ref, o_ref, sa0, sa1, sb0, sb1, *, tq):
    i = pl.program_id(2)
    key = lax.broadcasted_iota(jnp.int32, (tq, tq), 0)
    qry = lax.broadcasted_iota(jnp.int32, (tq, tq), 1)
    heads = [slice(hh * HEAD_PAD, (hh + 1) * HEAD_PAD) for hh in range(2)]
    buf_a = (sa0, sa1)
    buf_b = (sb0, sb1)

    def scores(j, hh, dst):
        ks = pl.multiple_of(j * tq, tq)
        dst[hh][...] = _dot(k_ref[pl.ds(ks, tq), heads[hh]], qt_ref[heads[hh], :])

    def consume(j, hh, src, state, masked):
        m, acc = state
        ks = pl.multiple_of(j * tq, tq)
        s = src[hh][...]
        if masked:
            s = jnp.where(key <= qry, s, NEG_BIG)
        m_new = jnp.maximum(m, jnp.max(s, axis=0, keepdims=True))
        p = jnp.exp2(s - m_new).astype(BF16)
        acc = jnp.exp2(m - m_new) * acc + _dot(vt_ref[heads[hh], pl.ds(ks, tq)], p)
        return m_new, acc

    def step(j, src, dst, carry, masked=False, prefetch=True):
        out = []
        for hh in range(2):
            if prefetch:
                scores(j + 1, hh, dst)
            out.append(consume(j, hh, src, carry[hh], masked))
        return tuple(out)

    def finish(carry):
        for hh, (_, acc) in enumerate(carry):
            o_ref[hh * MLA_V:(hh + 1) * MLA_V, :] = (acc[:MLA_V] / acc[MLA_V:MLA_V + 1]).astype(BF16)

    for hh in range(2):
        scores(0, hh, buf_a)

    def two_steps(t, carry):
        carry = step(2 * t, buf_a, buf_b, carry)
        return step(2 * t + 1, buf_b, buf_a, carry)

    init = tuple((jnp.full((1, tq), NEG_BIG, F32), jnp.zeros((HEAD_PAD, tq), F32)) for _ in heads)
    carry = lax.fori_loop(0, i // 2, two_steps, init)

    @pl.when(i % 2 == 0)
    def _():
        finish(step(i, buf_a, buf_b, carry, masked=True, prefetch=False))

    @pl.when(i % 2 == 1)
    def _():
        c = step(i - 1, buf_a, buf_b, carry)
        finish(step(i, buf_b, buf_a, c, masked=True, prefetch=False))


def _attn(qt, k, vt, *, batch, seq, tq=512):
    T = k.shape[0]
    tq = min(tq, seq)
    nq = seq // tq
    pairs = MLA_HEADS // 2
    return pl.pallas_call(
        functools.partial(_attn_kernel, tq=tq),
        out_shape=jax.ShapeDtypeStruct((MLA_HEADS * MLA_V, T), BF16),
        grid=(batch, pairs, nq),
        in_specs=[pl.BlockSpec((2 * HEAD_PAD, tq), lambda b, p, i: (p, b * nq + i)),
                  pl.BlockSpec((seq, 2 * HEAD_PAD), lambda b, p, i: (b, p)),
                  pl.BlockSpec((2 * HEAD_PAD, seq), lambda b, p, i: (p, b))],
        out_specs=pl.BlockSpec((2 * MLA_V, tq), lambda b, p, i: (p, b * nq + i)),
        scratch_shapes=[pltpu.VMEM((tq, tq), F32)] * 4,
        compiler_params=_params(("parallel", "parallel", "arbitrary")),
        name="mla_attn",
    )(qt, k, vt)


def _merge_kernel(x_ref, omt_ref, oh_ref, g01_ref, pm_ref, womla_ref, wohg_ref, wout_ref, o_ref):
    D = x_ref.shape[1]
    y_mla = _dot_tn(omt_ref[...], womla_ref[...])
    y_hg = _dot(oh_ref[...], wohg_ref[...])
    merged = (g01_ref[:, :D].astype(F32) * y_mla + g01_ref[:, D:].astype(F32) * y_hg
              + pm_ref[...].astype(F32))
    o_ref[...] = x_ref[...] + _dot(merged.astype(BF16), wout_ref[...])


def _merge(x, o_mla_t, o_hg, g01, pm, womla, wohg, wout, layer, *, tm=512):
    T, D = x.shape
    tm = min(tm, T)
    row = lambda w: pl.BlockSpec((tm, w), lambda i: (i, 0))
    return pl.pallas_call(
        _merge_kernel,
        out_shape=jax.ShapeDtypeStruct((T, D), F32),
        grid=(T // tm,),
        in_specs=[row(D), pl.BlockSpec((o_mla_t.shape[0], tm), lambda i: (0, i)), row(o_hg.shape[1]),
                  row(2 * D), row(D), _layer_spec(womla, layer), _layer_spec(wohg, layer),
                  _layer_spec(wout, layer)],
        out_specs=row(D),
        compiler_params=_params(("parallel",)),
        name="merge",
    )(x, o_mla_t, o_hg, g01, pm, womla, wohg, wout)


def _prep_weights(w_in, w_uq, w_uk, w_uv, w_mem_kv):
    L, D, _ = w_in.shape
    offs = np.cumsum([0, Q_LORA, KV_LORA, MLA_ROPE, HG_W, HG_W, HG_W, HG_W, MEM_W, N_BRANCH * D])
    kr = w_in[:, :, offs[2]:offs[3]]
    kr_group = jnp.concatenate([jnp.zeros((L, D, MLA_NOPE), F32), kr,
                                jnp.zeros((L, D, HEAD_PAD - MLA_QK), F32)], axis=2)
    wa = jnp.concatenate([w_in[:, :, :offs[2]], kr_group], axis=2)
    wh = w_in[:, :, offs[3]:offs[7]]
    wm = w_in[:, :, offs[7]:offs[8]] * (MEM_HDIM ** -0.5)
    wg = w_in[:, :, offs[8]:offs[9]]
    uq = (w_uq * (MLA_QK ** -0.5 * LOG2E)).reshape(L, Q_LORA, MLA_HEADS, MLA_QK)
    uq = jnp.pad(uq, ((0, 0), (0, 0), (0, 0), (0, HEAD_PAD - MLA_QK)))
    uk = w_uk.reshape(L, KV_LORA, MLA_HEADS, MLA_NOPE)
    uk = jnp.pad(uk, ((0, 0), (0, 0), (0, 0), (0, HEAD_PAD - MLA_NOPE)))
    uv = w_uv.reshape(L, KV_LORA, MLA_HEADS, MLA_V)
    uv = jnp.pad(uv, ((0, 0), (0, 0), (0, 0), (0, HEAD_PAD - MLA_V)))
    G = MLA_HEADS * HEAD_PAD
    wkt = jnp.swapaxes(w_mem_kv[:, :, :MEM_W], 1, 2)
    wv = w_mem_kv[:, :, MEM_W:]
    cast = lambda t: t.astype(BF16)
    return dict(wa=cast(wa), wh=cast(wh), wm=cast(wm), wg=cast(wg),
                wuqt=cast(jnp.swapaxes(uq.reshape(L, Q_LORA, G), 1, 2)),
                wuk=cast(uk.reshape(L, KV_LORA, G)),
                wuvt=cast(jnp.swapaxes(uv.reshape(L, KV_LORA, G), 1, 2)),
                wkt=cast(wkt), wv=cast(wv))


def kernel(x, mem, positions, ffn1_norm, w_ffn1_in, w_ffn1_out, mix_norm, w_in, q_lat_norm, kv_lat_norm,
           w_uq, w_uk, w_uv, w_o_mla, hg_lower_bounds, hg_out_norm, w_o_hg, mem_norm, w_mem_kv, w_o_mem,
           w_out, ffn2_norm, w_ffn2_in, w_ffn2_out, final_norm):
    B, S, D = x.shape
    L = w_in.shape[0]
    T = B * S
    vec = lambda t: t.reshape(L, 1, -1)
    pw = _prep_weights(w_in, w_uq, w_uk, w_uv, w_mem_kv)
    w1a, w1b = w_ffn1_in.astype(BF16), w_ffn1_out.astype(BF16)
    w2a, w2b = w_ffn2_in.astype(BF16), w_ffn2_out.astype(BF16)
    womla, wohg, wout = w_o_mla.astype(BF16), w_o_hg.astype(BF16), w_out.astype(BF16)
    stacked = [vec(mix_norm), pw["wa"], pw["wh"], pw["wm"], pw["wg"], vec(q_lat_norm), vec(kv_lat_norm),
               pw["wuqt"], pw["wuk"], pw["wuvt"], w_o_mem.astype(BF16)]

    rows, cols = _rope_tables(positions)
    lb, onemlb = _lower_bounds(hg_lower_bounds)
    stacked += [vec(lb), vec(onemlb), vec(hg_out_norm)]
    kt_all, vm_all = _memkv(mem, mem_norm, pw["wkt"], pw["wv"])

    xf = x.reshape(T, D)
    for l in range(L):
        xf = _ffn(xf, vec(ffn1_norm), w1a, w1b, final_norm, l, final_norm=False)
        qt, k, vt, o_hg, g01, pm = _mixer(xf, stacked, l, rows, cols, kt_all, vm_all, seq=S)
        o_mla_t = _attn(qt, k, vt, batch=B, seq=S)
        xf = _merge(xf, o_mla_t, o_hg, g01, pm, womla, wohg, wout, l)
        xf = _ffn(xf, vec(ffn2_norm), w2a, w2b, final_norm, l, final_norm=(l == L - 1))
    return xf.reshape(B, S, D)
```

```python
import functools
import math

import numpy as np
import jax
import jax.numpy as jnp
from jax import lax
from jax.experimental import pallas as pl
from jax.experimental.pallas import tpu as pltpu

F32 = jnp.float32
BF16 = jnp.bfloat16

MLA_HEADS = 8
MLA_NOPE = 64
MLA_ROPE = 32
MLA_V = 64
Q_LORA = 384
KV_LORA = 256
ROPE_THETA = 10000.0
HG_HEADS = 4
HG_KDIM = 128
HG_VDIM = 128
MEM_HEADS = 4
MEM_HDIM = 128
N_BRANCH = 3
NORM_EPS = 1e-6
MLA_QK = MLA_NOPE + MLA_ROPE
HG_W = HG_HEADS * HG_KDIM
MEM_W = MEM_HEADS * MEM_HDIM

LANES = 128
SUBLANES = 8
MXU_TILE = 256
VMEM_LIMIT_BYTES = 56 * 1024 * 1024

HEAD_PAD = LANES
ROPE_HALF = MLA_ROPE // 2
NEG_BIG = -1e30
LOG2E = math.log2(math.e)


def _rms(x, g):
    ms = jnp.mean(x * x, axis=-1, keepdims=True)
    return x * lax.rsqrt(ms + NORM_EPS) * g


def _sigmoid(x):
    return 1.0 / (1.0 + jnp.exp(-x))


def _dot(a, b):
    return jnp.dot(a, b, preferred_element_type=F32)


def _dot_nt(a, b):
    return lax.dot_general(a, b, (((1,), (1,)), ((), ())), preferred_element_type=F32)


def _dot_tn(a, b):
    return lax.dot_general(a, b, (((0,), (0,)), ((), ())), preferred_element_type=F32)


def _const_spec(shape):
    nd = len(shape)
    return pl.BlockSpec(shape, lambda *_: (0,) * nd, pipeline_mode=pl.Buffered(1))


def _layer_spec(arr, layer):
    nd = arr.ndim - 1
    return pl.BlockSpec((None,) + arr.shape[1:], lambda *_: (layer,) + (0,) * nd,
                        pipeline_mode=pl.Buffered(1))


def _params(sem):
    return pltpu.CompilerParams(dimension_semantics=sem, vmem_limit_bytes=VMEM_LIMIT_BYTES)


def _rope_row_kernel(pos_ref, freq_ref, cf_ref, s1_ref, s2_ref):
    ang = pos_ref[...].astype(F32) * freq_ref[...]
    c = jnp.cos(ang)
    s = jnp.sin(ang)
    lane = lax.broadcasted_iota(jnp.int32, ang.shape, 1)
    lo = (lane >= MLA_NOPE) & (lane < MLA_NOPE + ROPE_HALF)
    hi = (lane >= MLA_NOPE + ROPE_HALF) & (lane < MLA_QK)
    cf_ref[...] = jnp.where(lane < MLA_NOPE, 1.0, jnp.where(lane < MLA_QK, c, 0.0))
    s1_ref[...] = jnp.where(lo, -s, 0.0)
    s2_ref[...] = jnp.where(hi, s, 0.0)


def _rope_col_kernel(pos_ref, freq_ref, cos_ref, sin_ref):
    ang = freq_ref[...] * pos_ref[...].astype(F32)
    cos_ref[...] = jnp.cos(ang)
    sin_ref[...] = jnp.sin(ang)


def _rope_tables(positions):
    T = positions.size
    tm = min(T, 1024)
    inv = ROPE_THETA ** (-jnp.arange(0, MLA_ROPE, 2, dtype=F32) / MLA_ROPE)
    freq = jnp.concatenate([jnp.zeros((MLA_NOPE,), F32), inv, inv,
                            jnp.zeros((LANES - MLA_QK,), F32)]).reshape(1, LANES)
    out = jax.ShapeDtypeStruct((T, LANES), F32)
    row = pl.BlockSpec((tm, LANES), lambda i: (i, 0))
    rows = pl.pallas_call(
        _rope_row_kernel,
        out_shape=(out, out, out),
        grid=(T // tm,),
        in_specs=[pl.BlockSpec((tm, 1), lambda i: (i, 0)), _const_spec((1, LANES))],
        out_specs=(row, row, row),
        compiler_params=_params(("parallel",)),
        name="rope_rows",
    )(positions.reshape(T, 1), freq)
    outc = jax.ShapeDtypeStruct((ROPE_HALF, T), F32)
    col = pl.BlockSpec((ROPE_HALF, tm), lambda i: (0, i))
    cols = pl.pallas_call(
        _rope_col_kernel,
        out_shape=(outc, outc),
        grid=(T // tm,),
        in_specs=[pl.BlockSpec((1, tm), lambda i: (0, i)), _const_spec((ROPE_HALF, 1))],
        out_specs=(col, col),
        compiler_params=_params(("parallel",)),
        name="rope_cols",
    )(positions.reshape(1, T), inv.reshape(ROPE_HALF, 1))
    return rows, cols


def _lower_bound_kernel(x_ref, lb_ref, onemlb_ref):
    x = x_ref[...]
    depth = x.shape[0]
    e = jnp.exp(x - jnp.max(x, axis=0, keepdims=True))
    p = e / jnp.sum(e, axis=0, keepdims=True)
    acc = jnp.zeros_like(p[0:1])
    for l in range(depth):
        if l > 0:
            acc = acc + p[l:l + 1]
        lb_ref[l:l + 1, :] = acc
        onemlb_ref[l:l + 1, :] = 1.0 - acc


def _lower_bounds(hg_lower_bounds):
    out = jax.ShapeDtypeStruct(hg_lower_bounds.shape, F32)
    return pl.pallas_call(_lower_bound_kernel, out_shape=(out, out),
                          name="hgrn_lower_bounds")(hg_lower_bounds.astype(F32))


def _memkv_kernel(mem_ref, g_ref, wkt_ref, wv_ref, kt_ref, v_ref):
    mn = _rms(mem_ref[...], g_ref[...]).astype(BF16)
    kt_ref[...] = _dot_nt(wkt_ref[...], mn).astype(BF16)
    v_ref[...] = _dot(mn, wv_ref[...]).astype(BF16)


def _memkv(mem, mem_norm, wkt, wv):
    B, M, D = mem.shape
    L = mem_norm.shape[0]
    return pl.pallas_call(
        _memkv_kernel,
        out_shape=(jax.ShapeDtypeStruct((L, B, MEM_W, M), BF16),
                   jax.ShapeDtypeStruct((L, B, M, MEM_W), BF16)),
        grid=(L, B),
        in_specs=[pl.BlockSpec((None, M, D), lambda l, b: (b, 0, 0)),
                  pl.BlockSpec((None, 1, D), lambda l, b: (l, 0, 0)),
                  pl.BlockSpec((None, MEM_W, D), lambda l, b: (l, 0, 0)),
                  pl.BlockSpec((None, D, MEM_W), lambda l, b: (l, 0, 0))],
        out_specs=(pl.BlockSpec((None, None, MEM_W, M), lambda l, b: (l, b, 0, 0)),
                   pl.BlockSpec((None, None, M, MEM_W), lambda l, b: (l, b, 0, 0))),
        compiler_params=_params(("parallel", "parallel")),
        name="mem_kv",
    )(mem, mem_norm.reshape(L, 1, D), wkt, wv)


def _ffn_chunks(d_ff, n_chunks):
    tiles = d_ff // MXU_TILE
    assert tiles * MXU_TILE == d_ff
    bounds = [MXU_TILE * ((tiles * c) // n_chunks) for c in range(n_chunks + 1)]
    return list(zip(bounds[:-1], bounds[1:]))


def _ffn_kernel(x_ref, g_ref, w1_ref, w2_ref, gf_ref, o_ref, *, d_ff, n_chunks, final_norm):
    x = x_ref[...]
    xn = _rms(x, g_ref[...]).astype(BF16)
    acc = None
    for lo, hi in _ffn_chunks(d_ff, n_chunks):
        a = _dot(xn, w1_ref[:, lo:hi])
        b = _dot(xn, w1_ref[:, d_ff + lo:d_ff + hi])
        h = (a * _sigmoid(a) * b).astype(BF16)
        part = _dot(h, w2_ref[lo:hi, :])
        acc = part if acc is None else acc + part
    y = x + 0.5 * acc
    if final_norm:
        y = _rms(y, gf_ref[...])
    o_ref[...] = y


def _ffn(x, g, w1, w2, gf, layer, *, final_norm, tm=512, n_chunks=2):
    T, D = x.shape
    d_ff = w2.shape[1]
    tm = min(tm, T)
    row = pl.BlockSpec((tm, D), lambda i: (i, 0))
    return pl.pallas_call(
        functools.partial(_ffn_kernel, d_ff=d_ff, n_chunks=n_chunks, final_norm=final_norm),
        out_shape=jax.ShapeDtypeStruct((T, D), F32),
        grid=(T // tm,),
        in_specs=[row, _layer_spec(g, layer), _layer_spec(w1, layer), _layer_spec(w2, layer),
                  _const_spec((1, D))],
        out_specs=row,
        compiler_params=_params(("parallel",)),
        name="ffn",
    )(x, g, w1, w2, gf.reshape(1, D))


def _level_map(n):
    t = np.arange(n)[:, None]
    s = np.arange(n)[None, :]
    x = np.bitwise_xor(t, s)
    lvl = np.floor(np.log2(np.maximum(x, 1))).astype(np.int32)
    out = np.where(s < t, lvl, -1)
    out = np.where(s == t, int(np.log2(n)), out)
    return out.astype(np.int32)


def _hgrn_stages(h_scr, r0, tc, lb_ref, onemlb_ref, onorm_ref, tri_ref, lvl_ref, o_ref, st_ref, g_scr):
    half = tc // 2
    nlev = int(np.log2(half))
    W = HG_W
    rs_all = slice(r0, r0 + tc)

    z = h_scr[rs_all, HG_W:2 * HG_W]
    lb = lb_ref[...]
    e = jnp.exp(-jnp.abs(z))
    pos = z >= 0.0
    lden = jnp.log(1.0 + e)
    g = jnp.log(jnp.where(pos, 1.0 + lb * e, lb + e)) - lden
    g = jnp.maximum(g, jnp.minimum(z, 0.0) - lden) * LOG2E
    kk = onemlb_ref[...] * jnp.where(pos, e, 1.0) / (1.0 + e)
    hq = h_scr[rs_all, :HG_W]
    qq = hq * _sigmoid(hq)

    g_hi = g.astype(BF16)
    r1 = g - g_hi.astype(F32)
    g_mid = r1.astype(BF16)
    g_lo = (r1 - g_mid.astype(F32)).astype(BF16)
    tri = tri_ref[...]
    G = _dot(tri, g_hi) + _dot(tri, g_mid) + _dot(tri, g_lo)
    g_scr[...] = G
    g_last = g_scr[tc - 1:tc, :]
    qb = qq.astype(BF16)
    kb = kk.astype(BF16)
    yield

    rows = lax.broadcasted_iota(jnp.int32, G.shape, 0)
    qe = []
    ke = []
    for lv in range(nlev + 1):
        m = 1 << lv
        if lv == 0:
            d = jnp.where((rows & 1) != 0, g, 0.0)
        elif m >= SUBLANES:
            parts = []
            for b in range(tc // (2 * m)):
                lo = b * 2 * m
                ref_row = g_scr[lo + m - 1:lo + m, :]
                parts.append(ref_row - G[lo:lo + m])
                parts.append(G[lo + m:lo + 2 * m] - ref_row)
            d = jnp.concatenate(parts, axis=0)
        else:
            sub = lax.broadcasted_iota(jnp.int32, (SUBLANES, W), 0)
            parts = []
            for b in range(tc // SUBLANES):
                r = None
                for j in reversed(range(SUBLANES // (2 * m))):
                    rr = b * SUBLANES + j * 2 * m + m - 1
                    rowv = jnp.broadcast_to(g_scr[rr:rr + 1, :], (SUBLANES, W))
                    r = rowv if r is None else jnp.where(sub < (j + 1) * 2 * m, rowv, r)
                parts.append(r)
            d = -jnp.abs(G - jnp.concatenate(parts, axis=0))
        ed = jnp.exp2(d.astype(BF16))
        qe.append(qb * ed)
        ke.append(kb * ed)
        if lv % 2 == 1:
            yield

    q_in = (qq * jnp.exp2(G)).astype(BF16)
    k_out = (kk * jnp.exp2(g_last - G)).astype(BF16)
    d_last = jnp.exp2(g_last)
    vv = h_scr[rs_all, 2 * HG_W:3 * HG_W].astype(BF16)
    gate = h_scr[rs_all, 3 * HG_W:]
    lvl = lvl_ref[...]
    yield
    zero = jnp.zeros((half, HG_KDIM), BF16)

    def pair_products(q_rows, k_rows, psl):
        ka = k_rows[:, psl.start:psl.start + HG_KDIM]
        kb2 = k_rows[:, psl.start + HG_KDIM:psl.stop]
        kbd = jnp.concatenate([jnp.concatenate([ka, zero], axis=1),
                               jnp.concatenate([zero, kb2], axis=1)], axis=0)
        return _dot_nt(q_rows[:, psl], kbd)

    for pair in range(HG_HEADS // 2):
        psl = slice(2 * pair * HG_KDIM, (2 * pair + 2) * HG_KDIM)
        diag = []
        for b0 in (0, half):
            rs = slice(b0, b0 + half)
            A = jnp.where(lvl == nlev, pair_products(qb[rs], kb[rs], psl), 0.0)
            for lv in range(nlev):
                A = jnp.where(lvl == lv, pair_products(qe[lv][rs], ke[lv][rs], psl), A)
            diag.append(A.astype(BF16))
        cross = pair_products(qe[nlev][half:], ke[nlev][:half], psl).astype(BF16)
        yield
        for hh in range(2):
            h = 2 * pair + hh
            sl = slice(h * HG_KDIM, (h + 1) * HG_KDIM)
            hs = slice(hh * HG_KDIM, (hh + 1) * HG_KDIM)
            st = st_ref[h]
            v = vv[:, sl]
            o_int = _dot_nt(q_in[:, sl], st.astype(BF16))
            o_top = _dot(diag[0][:, hs], v[:half]) + o_int[:half]
            o_bot = _dot(cross[:, hs], v[:half]) + _dot(diag[1][:, hs], v[half:]) + o_int[half:]
            st_ref[h] = st * d_last[:, sl] + _dot_tn(v, k_out[:, sl])
            gt = gate[:, sl]
            sg = gt * _sigmoid(gt)
            o_ref[r0:r0 + half, sl] = (_rms(o_top, onorm_ref[...]) * sg[:half]).astype(BF16)
            o_ref[r0 + half:r0 + tc, sl] = (_rms(o_bot, onorm_ref[...]) * sg[half:]).astype(BF16)
        yield


def _proj_stages(u, wa_ref, wm_ref, wg_ref, qn_ref, kvn_ref, wuqt_ref, wuk_ref, wuvt_ref, wom_ref,
                 cf_ref, s1_ref, s2_ref, cos_ref, sin_ref, kt_ref, vm_ref,
                 qt_ref, k_ref, vt_ref, g01_ref, pm_ref):
    D = u.shape[1]
    za = _dot_nt(u, wa_ref[...])
    cq = _rms(za[:, :Q_LORA], qn_ref[...]).astype(BF16)
    ckv = _rms(za[:, Q_LORA:Q_LORA + KV_LORA], kvn_ref[...]).astype(BF16)
    t = za[:, Q_LORA + KV_LORA:]
    kr = (t * cf_ref[...] + pltpu.roll(t, LANES - ROPE_HALF, axis=1) * s1_ref[...]
          + pltpu.roll(t, ROPE_HALF, axis=1) * s2_ref[...])
    k = _dot(ckv, wuk_ref[...])
    for h in range(MLA_HEADS):
        sl = slice(h * HEAD_PAD, (h + 1) * HEAD_PAD)
        k_ref[:, sl] = (k[:, sl] + kr).astype(BF16)
    yield

    qt = _dot_nt(wuqt_ref[...], cq)
    cos = cos_ref[...]
    sin = sin_ref[...]
    for h in range(MLA_HEADS):
        r0 = h * HEAD_PAD
        t1 = qt[r0 + MLA_NOPE:r0 + MLA_NOPE + ROPE_HALF]
        t2 = qt[r0 + MLA_NOPE + ROPE_HALF:r0 + MLA_QK]
        qt_ref[r0:r0 + MLA_NOPE, :] = qt[r0:r0 + MLA_NOPE].astype(BF16)
        qt_ref[r0 + MLA_NOPE:r0 + MLA_NOPE + ROPE_HALF, :] = (t1 * cos - t2 * sin).astype(BF16)
        qt_ref[r0 + MLA_NOPE + ROPE_HALF:r0 + MLA_QK, :] = (t2 * cos + t1 * sin).astype(BF16)
        qt_ref[r0 + MLA_QK:r0 + HEAD_PAD, :] = jnp.zeros((HEAD_PAD - MLA_QK, qt.shape[1]), BF16)
    yield
    vt = _dot_nt(wuvt_ref[...], ckv)
    rowi = lax.broadcasted_iota(jnp.int32, vt.shape, 0)
    vt_ref[...] = jnp.where((rowi & (HEAD_PAD - 1)) == MLA_V, 1.0, vt).astype(BF16)
    yield

    mq = _dot_nt(u, wm_ref[...]).astype(BF16)
    heads = []
    for h in range(MEM_HEADS):
        sl = slice(h * MEM_HDIM, (h + 1) * MEM_HDIM)
        s = _dot(mq[:, sl], kt_ref[sl, :])
        p = jnp.exp(s - jnp.max(s, axis=-1, keepdims=True))
        o = _dot(p.astype(BF16), vm_ref[:, sl])
        heads.append((o / jnp.sum(p, axis=-1, keepdims=True)).astype(BF16))
        if h % 2 == 1:
            yield
    y_mem = _dot(jnp.concatenate(heads, axis=1), wom_ref[...])

    for b in range(N_BRANCH):
        gate = _sigmoid(_dot_nt(u, wg_ref[b * D:(b + 1) * D, :]))
        if b < N_BRANCH - 1:
            g01_ref[:, b * D:(b + 1) * D] = gate.astype(BF16)
        else:
            pm_ref[...] = (gate * y_mem).astype(BF16)
        yield


def _mixer_kernel(x_ref, g_ref, wa_ref, wh_ref, wm_ref, wg_ref, qn_ref, kvn_ref, wuqt_ref, wuk_ref,
                  wuvt_ref, wom_ref, lb_ref, onemlb_ref, onorm_ref, tri_ref, lvl_ref,
                  cf_ref, s1_ref, s2_ref, cos_ref, sin_ref, kt_ref, vm_ref,
                  qt_ref, k_ref, vt_ref, oh_ref, g01_ref, pm_ref, st_ref, g_scr, h_scr, *, tc, per_seq):
    @pl.when(pl.program_id(0) % per_seq == 0)
    def _():
        st_ref[...] = jnp.zeros_like(st_ref)

    u = _rms(x_ref[...], g_ref[...]).astype(BF16)
    for c in range(4):
        sl = slice(c * HG_W, (c + 1) * HG_W)
        h_scr[:, sl] = _dot_nt(u, wh_ref[sl, :])

    proj = _proj_stages(u, wa_ref, wm_ref, wg_ref, qn_ref, kvn_ref, wuqt_ref, wuk_ref, wuvt_ref, wom_ref,
                        cf_ref, s1_ref, s2_ref, cos_ref, sin_ref, kt_ref, vm_ref,
                        qt_ref, k_ref, vt_ref, g01_ref, pm_ref)
    tm = x_ref.shape[0]
    recur = (stage for r0 in range(0, tm, tc)
             for stage in _hgrn_stages(h_scr, r0, tc, lb_ref, onemlb_ref, onorm_ref, tri_ref, lvl_ref,
                                       oh_ref, st_ref, g_scr))
    live = [recur, recur, proj]
    while live:
        for gen in list(live):
            if gen in live and next(gen, StopIteration) is StopIteration:
                live = [g for g in live if g is not gen]


def _mixer(x, stacked, layer, rows, cols, kt, vm, *, seq, tm=512, tc=256):
    T, D = x.shape
    tm = min(tm, seq)
    tc = min(tc, tm)
    per_seq = seq // tm
    M = vm.shape[2]
    G = MLA_HEADS * HEAD_PAD
    tri = jnp.asarray(np.tril(np.ones((tc, tc), np.float32)), BF16)
    lvl = jnp.asarray(np.tile(_level_map(tc // 2), (1, 2)))
    row = lambda w: pl.BlockSpec((tm, w), lambda i: (i, 0))
    col = lambda h: pl.BlockSpec((h, tm), lambda i: (0, i))
    return pl.pallas_call(
        functools.partial(_mixer_kernel, tc=tc, per_seq=per_seq),
        out_shape=(jax.ShapeDtypeStruct((G, T), BF16),
                   jax.ShapeDtypeStruct((T, G), BF16),
                   jax.ShapeDtypeStruct((G, T), BF16),
                   jax.ShapeDtypeStruct((T, HG_W), BF16),
                   jax.ShapeDtypeStruct((T, 2 * D), BF16),
                   jax.ShapeDtypeStruct((T, D), BF16)),
        grid=(T // tm,),
        in_specs=[row(D)] + [_layer_spec(w, layer) for w in stacked]
                 + [_const_spec((tc, tc)), _const_spec((tc // 2, tc)),
                    row(LANES), row(LANES), row(LANES), col(ROPE_HALF), col(ROPE_HALF),
                    pl.BlockSpec((None, None, MEM_W, M), lambda i: (layer, i // per_seq, 0, 0)),
                    pl.BlockSpec((None, None, M, MEM_W), lambda i: (layer, i // per_seq, 0, 0))],
        out_specs=(col(G), row(G), col(G), row(HG_W), row(2 * D), row(D)),
        scratch_shapes=[pltpu.VMEM((HG_HEADS, HG_VDIM, HG_KDIM), F32), pltpu.VMEM((tc, HG_W), F32),
                        pltpu.VMEM((tm, 4 * HG_W), F32)],
        compiler_params=_params(("arbitrary",)),
        name="mixer_proj_hgrn",
    )(x, *stacked, tri, lvl, *rows, *cols, kt, vm)


def _attn_kernel(qt_ref, k_ref, vt_ref, o_ref, sa0, sa1, sb0, sb1, *, tq, nq):
    key = lax.broadcasted_iota(jnp.int32, (tq, tq), 0)
    qry = lax.broadcasted_iota(jnp.int32, (tq, tq), 1)
    heads = [slice(hh * HEAD_PAD, (hh + 1) * HEAD_PAD) for hh in range(2)]
    bufs = ((sa0, sa1), (sb0, sb1))
    tiles = [(i, j) for i in range(nq) for j in range(i + 1)]

    def scores(n, hh):
        i, j = tiles[n]
        bufs[n % 2][hh][...] = _dot(k_ref[j * tq:(j + 1) * tq, heads[hh]],
                                    qt_ref[heads[hh], i * tq:(i + 1) * tq])

    def consume(n, hh, state):
        i, j = tiles[n]
        m, acc = state
        s = bufs[n % 2][hh][...]
        if j == i:
            s = jnp.where(key <= qry, s, NEG_BIG)
        m_new = jnp.maximum(m, jnp.max(s, axis=0, keepdims=True))
        p = jnp.exp2(s - m_new).astype(BF16)
        acc = jnp.exp2(m - m_new) * acc + _dot(vt_ref[heads[hh], j * tq:(j + 1) * tq], p)
        return m_new, acc

    for hh in range(2):
        scores(0, hh)
    state = None
    for n, (i, j) in enumerate(tiles):
        if j == 0:
            state = [(jnp.full((1, tq), NEG_BIG, F32), jnp.zeros((HEAD_PAD, tq), F32)) for _ in heads]
        for hh in range(2):
            if n + 1 < len(tiles):
                scores(n + 1, hh)
            state[hh] = consume(n, hh, state[hh])
        if j == i:
            for hh, (_, acc) in enumerate(state):
                o_ref[hh * MLA_V:(hh + 1) * MLA_V, i * tq:(i + 1) * tq] = (
                    acc[:MLA_V] / acc[MLA_V:MLA_V + 1]).astype(BF16)


def _attn(qt, k, vt, *, batch, seq, tq=512):
    T = k.shape[0]
    tq = min(tq, seq)
    pairs = MLA_HEADS // 2
    return pl.pallas_call(
        functools.partial(_attn_kernel, tq=tq, nq=seq // tq),
        out_shape=jax.ShapeDtypeStruct((MLA_HEADS * MLA_V, T), BF16),
        grid=(batch, pairs),
        in_specs=[pl.BlockSpec((2 * HEAD_PAD, seq), lambda b, p: (p, b)),
                  pl.BlockSpec((seq, 2 * HEAD_PAD), lambda b, p: (b, p)),
                  pl.BlockSpec((2 * HEAD_PAD, seq), lambda b, p: (p, b))],
        out_specs=pl.BlockSpec((2 * MLA_V, seq), lambda b, p: (p, b)),
        scratch_shapes=[pltpu.VMEM((tq, tq), F32)] * 4,
        compiler_params=_params(("parallel", "parallel")),
        name="mla_attn",
    )(qt, k, vt)


def _merge_kernel(x_ref, omt_ref, oh_ref, g01_ref, pm_ref, womla_ref, wohg_ref, wout_ref, o_ref):
    D = x_ref.shape[1]
    y_mla = _dot_tn(omt_ref[...], womla_ref[...])
    y_hg = _dot(oh_ref[...], wohg_ref[...])
    merged = (g01_ref[:, :D].astype(F32) * y_mla + g01_ref[:, D:].astype(F32) * y_hg
              + pm_ref[...].astype(F32))
    o_ref[...] = x_ref[...] + _dot(merged.astype(BF16), wout_ref[...])


def _merge(x, o_mla_t, o_hg, g01, pm, womla, wohg, wout, layer, *, tm=512):
    T, D = x.shape
    tm = min(tm, T)
    row = lambda w: pl.BlockSpec((tm, w), lambda i: (i, 0))
    return pl.pallas_call(
        _merge_kernel,
        out_shape=jax.ShapeDtypeStruct((T, D), F32),
        grid=(T // tm,),
        in_specs=[row(D), pl.BlockSpec((o_mla_t.shape[0], tm), lambda i: (0, i)), row(o_hg.shape[1]),
                  row(2 * D), row(D), _layer_spec(womla, layer), _layer_spec(wohg, layer),
                  _layer_spec(wout, layer)],
        out_specs=row(D),
        compiler_params=_params(("parallel",)),
        name="merge",
    )(x, o_mla_t, o_hg, g01, pm, womla, wohg, wout)


def _prep_weights(w_in, w_uq, w_uk, w_uv, w_mem_kv):
    L, D, _ = w_in.shape
    offs = np.cumsum([0, Q_LORA, KV_LORA, MLA_ROPE, HG_W, HG_W, HG_W, HG_W, MEM_W, N_BRANCH * D])
    wt = jnp.swapaxes(w_in, 1, 2)
    kr = wt[:, offs[2]:offs[3]]
    kr_group = jnp.concatenate([jnp.zeros((L, MLA_NOPE, D), F32), kr,
                                jnp.zeros((L, HEAD_PAD - MLA_QK, D), F32)], axis=1)
    wa = jnp.concatenate([wt[:, :offs[2]], kr_group], axis=1)
    wh = wt[:, offs[3]:offs[7]]
    wm = wt[:, offs[7]:offs[8]] * (MEM_HDIM ** -0.5)
    wg = wt[:, offs[8]:offs[9]]
    uq = (w_uq * (MLA_QK ** -0.5 * LOG2E)).reshape(L, Q_LORA, MLA_HEADS, MLA_QK)
    uq = jnp.pad(uq, ((0, 0), (0, 0), (0, 0), (0, HEAD_PAD - MLA_QK)))
    uk = w_uk.reshape(L, KV_LORA, MLA_HEADS, MLA_NOPE)
    uk = jnp.pad(uk, ((0, 0), (0, 0), (0, 0), (0, HEAD_PAD - MLA_NOPE)))
    uv = w_uv.reshape(L, KV_LORA, MLA_HEADS, MLA_V)
    uv = jnp.pad(uv, ((0, 0), (0, 0), (0, 0), (0, HEAD_PAD - MLA_V)))
    G = MLA_HEADS * HEAD_PAD
    wkt = jnp.swapaxes(w_mem_kv[:, :, :MEM_W], 1, 2)
    wv = w_mem_kv[:, :, MEM_W:]
    cast = lambda t: t.astype(BF16)
    return dict(wa=cast(wa), wh=cast(wh), wm=cast(wm), wg=cast(wg),
                wuqt=cast(jnp.swapaxes(uq.reshape(L, Q_LORA, G), 1, 2)),
                wuk=cast(uk.reshape(L, KV_LORA, G)),
                wuvt=cast(jnp.swapaxes(uv.reshape(L, KV_LORA, G), 1, 2)),
                wkt=cast(wkt), wv=cast(wv))


def kernel(x, mem, positions, ffn1_norm, w_ffn1_in, w_ffn1_out, mix_norm, w_in, q_lat_norm, kv_lat_norm,
           w_uq, w_uk, w_uv, w_o_mla, hg_lower_bounds, hg_out_norm, w_o_hg, mem_norm, w_mem_kv, w_o_mem,
           w_out, ffn2_norm, w_ffn2_in, w_ffn2_out, final_norm):
    B, S, D = x.shape
    L = w_in.shape[0]
    T = B * S
    vec = lambda t: t.reshape(L, 1, -1)
    pw = _prep_weights(w_in, w_uq, w_uk, w_uv, w_mem_kv)
    w1a, w1b = w_ffn1_in.astype(BF16), w_ffn1_out.astype(BF16)
    w2a, w2b = w_ffn2_in.astype(BF16), w_ffn2_out.astype(BF16)
    womla, wohg, wout = w_o_mla.astype(BF16), w_o_hg.astype(BF16), w_out.astype(BF16)
    stacked = [vec(mix_norm), pw["wa"], pw["wh"], pw["wm"], pw["wg"], vec(q_lat_norm), vec(kv_lat_norm),
               pw["wuqt"], pw["wuk"], pw["wuvt"], w_o_mem.astype(BF16)]

    rows, cols = _rope_tables(positions)
    lb, onemlb = _lower_bounds(hg_lower_bounds)
    stacked += [vec(lb), vec(onemlb), vec(hg_out_norm)]
    kt_all, vm_all = _memkv(mem, mem_norm, pw["wkt"], pw["wv"])

    xf = x.reshape(T, D)
    for l in range(L):
        xf = _ffn(xf, vec(ffn1_norm), w1a, w1b, final_norm, l, final_norm=False)
        qt, k, vt, o_hg, g01, pm = _mixer(xf, stacked, l, rows, cols, kt_all, vm_all, seq=S)
        o_mla_t = _attn(qt, k, vt, batch=B, seq=S)
        xf = _merge(xf, o_mla_t, o_hg, g01, pm, womla, wohg, wout, l)
        xf = _ffn(xf, vec(ffn2_norm), w2a, w2b, final_norm, l, final_norm=(l == L - 1))
    return xf.reshape(B, S, D)
```

```python
import functools
import math

import numpy as np
import jax
import jax.numpy as jnp
from jax import lax
from jax.experimental import pallas as pl
from jax.experimental.pallas import tpu as pltpu

F32 = jnp.float32
BF16 = jnp.bfloat16

MLA_HEADS = 8
MLA_NOPE = 64
MLA_ROPE = 32
MLA_V = 64
Q_LORA = 384
KV_LORA = 256
ROPE_THETA = 10000.0
HG_HEADS = 4
HG_KDIM = 128
HG_VDIM = 128
MEM_HEADS = 4
MEM_HDIM = 128
N_BRANCH = 3
NORM_EPS = 1e-6
MLA_QK = MLA_NOPE + MLA_ROPE
HG_W = HG_HEADS * HG_KDIM
MEM_W = MEM_HEADS * MEM_HDIM

LANES = 128
SUBLANES = 8
MXU_TILE = 256
VMEM_LIMIT_BYTES = 56 * 1024 * 1024

HEAD_PAD = LANES
ROPE_HALF = MLA_ROPE // 2
NEG_BIG = -1e30
LOG2E = math.log2(math.e)


def _rms(x, g):
    ms = jnp.mean(x * x, axis=-1, keepdims=True)
    return x * lax.rsqrt(ms + NORM_EPS) * g


def _sigmoid(x):
    return 1.0 / (1.0 + jnp.exp(-x))


def _dot(a, b):
    return jnp.dot(a, b, preferred_element_type=F32)


def _dot_nt(a, b):
    return lax.dot_general(a, b, (((1,), (1,)), ((), ())), preferred_element_type=F32)


def _dot_tn(a, b):
    return lax.dot_general(a, b, (((0,), (0,)), ((), ())), preferred_element_type=F32)


def _const_spec(shape):
    nd = len(shape)
    return pl.BlockSpec(shape, lambda *_: (0,) * nd, pipeline_mode=pl.Buffered(1))


def _layer_spec(arr, layer):
    nd = arr.ndim - 1
    return pl.BlockSpec((None,) + arr.shape[1:], lambda *_: (layer,) + (0,) * nd,
                        pipeline_mode=pl.Buffered(1))


def _params(sem):
    return pltpu.CompilerParams(dimension_semantics=sem, vmem_limit_bytes=VMEM_LIMIT_BYTES)


def _rope_row_kernel(pos_ref, freq_ref, cf_ref, s1_ref, s2_ref):
    ang = pos_ref[...].astype(F32) * freq_ref[...]
    c = jnp.cos(ang)
    s = jnp.sin(ang)
    lane = lax.broadcasted_iota(jnp.int32, ang.shape, 1)
    lo = (lane >= MLA_NOPE) & (lane < MLA_NOPE + ROPE_HALF)
    hi = (lane >= MLA_NOPE + ROPE_HALF) & (lane < MLA_QK)
    cf_ref[...] = jnp.where(lane < MLA_NOPE, 1.0, jnp.where(lane < MLA_QK, c, 0.0))
    s1_ref[...] = jnp.where(lo, -s, 0.0)
    s2_ref[...] = jnp.where(hi, s, 0.0)


def _rope_col_kernel(pos_ref, freq_ref, cos_ref, sin_ref):
    ang = freq_ref[...] * pos_ref[...].astype(F32)
    cos_ref[...] = jnp.cos(ang)
    sin_ref[...] = jnp.sin(ang)


def _rope_tables(positions):
    T = positions.size
    tm = min(T, 1024)
    inv = ROPE_THETA ** (-jnp.arange(0, MLA_ROPE, 2, dtype=F32) / MLA_ROPE)
    freq = jnp.concatenate([jnp.zeros((MLA_NOPE,), F32), inv, inv,
                            jnp.zeros((LANES - MLA_QK,), F32)]).reshape(1, LANES)
    out = jax.ShapeDtypeStruct((T, LANES), F32)
    row = pl.BlockSpec((tm, LANES), lambda i: (i, 0))
    rows = pl.pallas_call(
        _rope_row_kernel,
        out_shape=(out, out, out),
        grid=(T // tm,),
        in_specs=[pl.BlockSpec((tm, 1), lambda i: (i, 0)), _const_spec((1, LANES))],
        out_specs=(row, row, row),
        compiler_params=_params(("parallel",)),
        name="rope_rows",
    )(positions.reshape(T, 1), freq)
    outc = jax.ShapeDtypeStruct((ROPE_HALF, T), F32)
    col = pl.BlockSpec((ROPE_HALF, tm), lambda i: (0, i))
    cols = pl.pallas_call(
        _rope_col_kernel,
        out_shape=(outc, outc),
        grid=(T // tm,),
        in_specs=[pl.BlockSpec((1, tm), lambda i: (0, i)), _const_spec((ROPE_HALF, 1))],
        out_specs=(col, col),
        compiler_params=_params(("parallel",)),
        name="rope_cols",
    )(positions.reshape(1, T), inv.reshape(ROPE_HALF, 1))
    return rows, cols


def _lower_bound_kernel(x_ref, lb_ref, onemlb_ref):
    x = x_ref[...]
    depth = x.shape[0]
    e = jnp.exp(x - jnp.max(x, axis=0, keepdims=True))
    p = e / jnp.sum(e, axis=0, keepdims=True)
    acc = jnp.zeros_like(p[0:1])
    for l in range(depth):
        if l > 0:
            acc = acc + p[l:l + 1]
        lb_ref[l:l + 1, :] = acc
        onemlb_ref[l:l + 1, :] = 1.0 - acc


def _lower_bounds(hg_lower_bounds):
    out = jax.ShapeDtypeStruct(hg_lower_bounds.shape, F32)
    return pl.pallas_call(_lower_bound_kernel, out_shape=(out, out),
                          name="hgrn_lower_bounds")(hg_lower_bounds.astype(F32))


def _memkv_kernel(mem_ref, g_ref, wkt_ref, wv_ref, kt_ref, v_ref):
    mn = _rms(mem_ref[...], g_ref[...]).astype(BF16)
    kt_ref[...] = _dot_nt(wkt_ref[...], mn).astype(BF16)
    v_ref[...] = _dot(mn, wv_ref[...]).astype(BF16)


def _memkv(mem, mem_norm, wkt, wv):
    B, M, D = mem.shape
    L = mem_norm.shape[0]
    return pl.pallas_call(
        _memkv_kernel,
        out_shape=(jax.ShapeDtypeStruct((L, B, MEM_W, M), BF16),
                   jax.ShapeDtypeStruct((L, B, M, MEM_W), BF16)),
        grid=(L, B),
        in_specs=[pl.BlockSpec((None, M, D), lambda l, b: (b, 0, 0)),
                  pl.BlockSpec((None, 1, D), lambda l, b: (l, 0, 0)),
                  pl.BlockSpec((None, MEM_W, D), lambda l, b: (l, 0, 0)),
                  pl.BlockSpec((None, D, MEM_W), lambda l, b: (l, 0, 0))],
        out_specs=(pl.BlockSpec((None, None, MEM_W, M), lambda l, b: (l, b, 0, 0)),
                   pl.BlockSpec((None, None, M, MEM_W), lambda l, b: (l, b, 0, 0))),
        compiler_params=_params(("parallel", "parallel")),
        name="mem_kv",
    )(mem, mem_norm.reshape(L, 1, D), wkt, wv)


def _ffn_chunks(d_ff, n_chunks):
    tiles = d_ff // MXU_TILE
    assert tiles * MXU_TILE == d_ff
    bounds = [MXU_TILE * ((tiles * c) // n_chunks) for c in range(n_chunks + 1)]
    return list(zip(bounds[:-1], bounds[1:]))


def _ffn_kernel(*refs, d_ff, n_chunks, final_norm, has_delta):
    if has_delta:
        x_ref, delta_ref, g_ref, w1_ref, w2_ref, gf_ref, o_ref = refs
        x = x_ref[...] + delta_ref[...].astype(F32)
    else:
        x_ref, g_ref, w1_ref, w2_ref, gf_ref, o_ref = refs
        x = x_ref[...]
    xn = _rms(x, g_ref[...]).astype(BF16)
    acc = None
    for lo, hi in _ffn_chunks(d_ff, n_chunks):
        a = _dot(xn, w1_ref[:, lo:hi])
        b = _dot(xn, w1_ref[:, d_ff + lo:d_ff + hi])
        h = (a * _sigmoid(a) * b).astype(BF16)
        part = _dot(h, w2_ref[lo:hi, :])
        acc = part if acc is None else acc + part
    y = x + 0.5 * acc
    if final_norm:
        y = _rms(y, gf_ref[...])
    o_ref[...] = y


def _ffn(x, delta, g, w1, w2, gf, layer, *, final_norm, tm=512, n_chunks=2):
    T, D = x.shape
    d_ff = w2.shape[1]
    tm = min(tm, T)
    row = pl.BlockSpec((tm, D), lambda i: (i, 0))
    acts = (x,) if delta is None else (x, delta)
    return pl.pallas_call(
        functools.partial(_ffn_kernel, d_ff=d_ff, n_chunks=n_chunks, final_norm=final_norm,
                          has_delta=delta is not None),
        out_shape=jax.ShapeDtypeStruct((T, D), F32),
        grid=(T // tm,),
        in_specs=[row] * len(acts) + [_layer_spec(g, layer), _layer_spec(w1, layer),
                                      _layer_spec(w2, layer), _const_spec((1, D))],
        out_specs=row,
        compiler_params=_params(("parallel",)),
        name="ffn",
    )(*acts, g, w1, w2, gf.reshape(1, D))


def _level_map(n):
    t = np.arange(n)[:, None]
    s = np.arange(n)[None, :]
    x = np.bitwise_xor(t, s)
    lvl = np.floor(np.log2(np.maximum(x, 1))).astype(np.int32)
    out = np.where(s < t, lvl, -1)
    out = np.where(s == t, int(np.log2(n)), out)
    return out.astype(np.int32)


def _hgrn_stages(h_scr, r0, tc, lb_ref, onemlb_ref, onorm_ref, tri_ref, lvl_ref, o_ref, st_ref, g_scr):
    half = tc // 2
    nlev = int(np.log2(half))
    W = HG_W
    rs_all = slice(r0, r0 + tc)

    z = h_scr[rs_all, HG_W:2 * HG_W]
    lb = lb_ref[...]
    e = jnp.exp(-jnp.abs(z))
    pos = z >= 0.0
    lden = jnp.log(1.0 + e)
    g = jnp.log(jnp.where(pos, 1.0 + lb * e, lb + e)) - lden
    g = jnp.maximum(g, jnp.minimum(z, 0.0) - lden) * LOG2E
    kk = onemlb_ref[...] * jnp.where(pos, e, 1.0) / (1.0 + e)
    hq = h_scr[rs_all, :HG_W]
    qq = hq * _sigmoid(hq)

    g_hi = g.astype(BF16)
    r1 = g - g_hi.astype(F32)
    g_mid = r1.astype(BF16)
    g_lo = (r1 - g_mid.astype(F32)).astype(BF16)
    tri = tri_ref[...]
    G = _dot(tri, g_hi) + _dot(tri, g_mid) + _dot(tri, g_lo)
    g_scr[...] = G
    g_last = g_scr[tc - 1:tc, :]
    qb = qq.astype(BF16)
    kb = kk.astype(BF16)
    yield

    rows = lax.broadcasted_iota(jnp.int32, G.shape, 0)
    qe = []
    ke = []
    for lv in range(nlev + 1):
        m = 1 << lv
        if lv == 0:
            d = jnp.where((rows & 1) != 0, g, 0.0)
        elif m >= SUBLANES:
            parts = []
            for b in range(tc // (2 * m)):
                lo = b * 2 * m
                ref_row = g_scr[lo + m - 1:lo + m, :]
                parts.append(ref_row - G[lo:lo + m])
                parts.append(G[lo + m:lo + 2 * m] - ref_row)
            d = jnp.concatenate(parts, axis=0)
        else:
            sub = lax.broadcasted_iota(jnp.int32, (SUBLANES, W), 0)
            parts = []
            for b in range(tc // SUBLANES):
                r = None
                for j in reversed(range(SUBLANES // (2 * m))):
                    rr = b * SUBLANES + j * 2 * m + m - 1
                    rowv = jnp.broadcast_to(g_scr[rr:rr + 1, :], (SUBLANES, W))
                    r = rowv if r is None else jnp.where(sub < (j + 1) * 2 * m, rowv, r)
                parts.append(r)
            d = -jnp.abs(G - jnp.concatenate(parts, axis=0))
        ed = jnp.exp2(d.astype(BF16))
        qe.append(qb * ed)
        ke.append(kb * ed)
        if lv % 2 == 1:
            yield

    q_in = (qq * jnp.exp2(G)).astype(BF16)
    k_out = (kk * jnp.exp2(g_last - G)).astype(BF16)
    d_last = jnp.exp2(g_last)
    vv = h_scr[rs_all, 2 * HG_W:3 * HG_W].astype(BF16)
    gate = h_scr[rs_all, 3 * HG_W:]
    lvl = lvl_ref[...]
    yield
    zero = jnp.zeros((half, HG_KDIM), BF16)

    def pair_products(q_rows, k_rows, psl):
        ka = k_rows[:, psl.start:psl.start + HG_KDIM]
        kb2 = k_rows[:, psl.start + HG_KDIM:psl.stop]
        kbd = jnp.concatenate([jnp.concatenate([ka, zero], axis=1),
                               jnp.concatenate([zero, kb2], axis=1)], axis=0)
        return _dot_nt(q_rows[:, psl], kbd)

    for pair in range(HG_HEADS // 2):
        psl = slice(2 * pair * HG_KDIM, (2 * pair + 2) * HG_KDIM)
        diag = []
        for b0 in (0, half):
            rs = slice(b0, b0 + half)
            A = jnp.where(lvl == nlev, pair_products(qb[rs], kb[rs], psl), 0.0)
            for lv in range(nlev):
                A = jnp.where(lvl == lv, pair_products(qe[lv][rs], ke[lv][rs], psl), A)
            diag.append(A.astype(BF16))
        cross = pair_products(qe[nlev][half:], ke[nlev][:half], psl).astype(BF16)
        yield
        for hh in range(2):
            h = 2 * pair + hh
            sl = slice(h * HG_KDIM, (h + 1) * HG_KDIM)
            hs = slice(hh * HG_KDIM, (hh + 1) * HG_KDIM)
            st = st_ref[h]
            v = vv[:, sl]
            o_int = _dot_nt(q_in[:, sl], st.astype(BF16))
            o_top = _dot(diag[0][:, hs], v[:half]) + o_int[:half]
            o_bot = _dot(cross[:, hs], v[:half]) + _dot(diag[1][:, hs], v[half:]) + o_int[half:]
            st_ref[h] = st * d_last[:, sl] + _dot_tn(v, k_out[:, sl])
            gt = gate[:, sl]
            sg = gt * _sigmoid(gt)
            o_ref[r0:r0 + half, sl] = (_rms(o_top, onorm_ref[...]) * sg[:half]).astype(BF16)
            o_ref[r0 + half:r0 + tc, sl] = (_rms(o_bot, onorm_ref[...]) * sg[half:]).astype(BF16)
        yield


def _proj_stages(u, wa_ref, wm_ref, wg_ref, qn_ref, kvn_ref, wuqt_ref, wuk_ref, wuvt_ref, wom_ref,
                 cf_ref, s1_ref, s2_ref, cos_ref, sin_ref, kt_ref, vm_ref,
                 qt_ref, k_ref, vt_ref, g01_ref, pm_ref):
    D = u.shape[1]
    za = _dot_nt(u, wa_ref[...])
    cq = _rms(za[:, :Q_LORA], qn_ref[...]).astype(BF16)
    ckv = _rms(za[:, Q_LORA:Q_LORA + KV_LORA], kvn_ref[...]).astype(BF16)
    t = za[:, Q_LORA + KV_LORA:]
    kr = (t * cf_ref[...] + pltpu.roll(t, LANES - ROPE_HALF, axis=1) * s1_ref[...]
          + pltpu.roll(t, ROPE_HALF, axis=1) * s2_ref[...])
    k = _dot(ckv, wuk_ref[...])
    for h in range(MLA_HEADS):
        sl = slice(h * HEAD_PAD, (h + 1) * HEAD_PAD)
        k_ref[:, sl] = (k[:, sl] + kr).astype(BF16)
    yield

    qt = _dot_nt(wuqt_ref[...], cq)
    cos = cos_ref[...]
    sin = sin_ref[...]
    for h in range(MLA_HEADS):
        r0 = h * HEAD_PAD
        t1 = qt[r0 + MLA_NOPE:r0 + MLA_NOPE + ROPE_HALF]
        t2 = qt[r0 + MLA_NOPE + ROPE_HALF:r0 + MLA_QK]
        qt_ref[r0:r0 + MLA_NOPE, :] = qt[r0:r0 + MLA_NOPE].astype(BF16)
        qt_ref[r0 + MLA_NOPE:r0 + MLA_NOPE + ROPE_HALF, :] = (t1 * cos - t2 * sin).astype(BF16)
        qt_ref[r0 + MLA_NOPE + ROPE_HALF:r0 + MLA_QK, :] = (t2 * cos + t1 * sin).astype(BF16)
        qt_ref[r0 + MLA_QK:r0 + HEAD_PAD, :] = jnp.zeros((HEAD_PAD - MLA_QK, qt.shape[1]), BF16)
    yield
    vt = _dot_nt(wuvt_ref[...], ckv)
    rowi = lax.broadcasted_iota(jnp.int32, vt.shape, 0)
    vt_ref[...] = jnp.where((rowi & (HEAD_PAD - 1)) == MLA_V, 1.0, vt).astype(BF16)
    yield

    mq = _dot_nt(u, wm_ref[...]).astype(BF16)
    heads = []
    for h in range(MEM_HEADS):
        sl = slice(h * MEM_HDIM, (h + 1) * MEM_HDIM)
        s = _dot(mq[:, sl], kt_ref[sl, :])
        p = jnp.exp(s - jnp.max(s, axis=-1, keepdims=True))
        o = _dot(p.astype(BF16), vm_ref[:, sl])
        heads.append((o / jnp.sum(p, axis=-1, keepdims=True)).astype(BF16))
        if h % 2 == 1:
            yield
    y_mem = _dot(jnp.concatenate(heads, axis=1), wom_ref[...])

    for b in range(N_BRANCH):
        gate = _sigmoid(_dot_nt(u, wg_ref[b * D:(b + 1) * D, :]))
        if b < N_BRANCH - 1:
            g01_ref[:, b * D:(b + 1) * D] = gate.astype(BF16)
        else:
            pm_ref[...] = (gate * y_mem).astype(BF16)
        yield


def _mixer_kernel(x_ref, g_ref, wa_ref, wh_ref, wm_ref, wg_ref, qn_ref, kvn_ref, wuqt_ref, wuk_ref,
                  wuvt_ref, wom_ref, lb_ref, onemlb_ref, onorm_ref, tri_ref, lvl_ref,
                  cf_ref, s1_ref, s2_ref, cos_ref, sin_ref, kt_ref, vm_ref,
                  qt_ref, k_ref, vt_ref, oh_ref, g01_ref, pm_ref, st_ref, g_scr, h_scr, *, tc, per_seq):
    @pl.when(pl.program_id(0) % per_seq == 0)
    def _():
        st_ref[...] = jnp.zeros_like(st_ref)

    u = _rms(x_ref[...], g_ref[...]).astype(BF16)
    for c in range(4):
        sl = slice(c * HG_W, (c + 1) * HG_W)
        h_scr[:, sl] = _dot_nt(u, wh_ref[sl, :])

    proj = _proj_stages(u, wa_ref, wm_ref, wg_ref, qn_ref, kvn_ref, wuqt_ref, wuk_ref, wuvt_ref, wom_ref,
                        cf_ref, s1_ref, s2_ref, cos_ref, sin_ref, kt_ref, vm_ref,
                        qt_ref, k_ref, vt_ref, g01_ref, pm_ref)
    tm = x_ref.shape[0]
    recur = (stage for r0 in range(0, tm, tc)
             for stage in _hgrn_stages(h_scr, r0, tc, lb_ref, onemlb_ref, onorm_ref, tri_ref, lvl_ref,
                                       oh_ref, st_ref, g_scr))
    live = [recur, recur, proj]
    while live:
        for gen in list(live):
            if gen in live and next(gen, StopIteration) is StopIteration:
                live = [g for g in live if g is not gen]


def _mixer(x, stacked, layer, rows, cols, kt, vm, *, seq, tm=512, tc=256):
    T, D = x.shape
    tm = min(tm, seq)
    tc = min(tc, tm)
    per_seq = seq // tm
    M = vm.shape[2]
    G = MLA_HEADS * HEAD_PAD
    tri = jnp.asarray(np.tril(np.ones((tc, tc), np.float32)), BF16)
    lvl = jnp.asarray(np.tile(_level_map(tc // 2), (1, 2)))
    row = lambda w: pl.BlockSpec((tm, w), lambda i: (i, 0))
    col = lambda h: pl.BlockSpec((h, tm), lambda i: (0, i))
    return pl.pallas_call(
        functools.partial(_mixer_kernel, tc=tc, per_seq=per_seq),
        out_shape=(jax.ShapeDtypeStruct((G, T), BF16),
                   jax.ShapeDtypeStruct((T, G), BF16),
                   jax.ShapeDtypeStruct((G, T), BF16),
                   jax.ShapeDtypeStruct((T, HG_W), BF16),
                   jax.ShapeDtypeStruct((T, 2 * D), BF16),
                   jax.ShapeDtypeStruct((T, D), BF16)),
        grid=(T // tm,),
        in_specs=[row(D)] + [_layer_spec(w, layer) for w in stacked]
                 + [_const_spec((tc, tc)), _const_spec((tc // 2, tc)),
                    row(LANES), row(LANES), row(LANES), col(ROPE_HALF), col(ROPE_HALF),
                    pl.BlockSpec((None, None, MEM_W, M), lambda i: (layer, i // per_seq, 0, 0)),
                    pl.BlockSpec((None, None, M, MEM_W), lambda i: (layer, i // per_seq, 0, 0))],
        out_specs=(col(G), row(G), col(G), row(HG_W), row(2 * D), row(D)),
        scratch_shapes=[pltpu.VMEM((HG_HEADS, HG_VDIM, HG_KDIM), F32), pltpu.VMEM((tc, HG_W), F32),
                        pltpu.VMEM((tm, 4 * HG_W), F32)],
        compiler_params=_params(("arbitrary",)),
        name="mixer_proj_hgrn",
    )(x, *stacked, tri, lvl, *rows, *cols, kt, vm)


def _attn_kernel(qt_ref, k_ref, vt_ref, o_ref, sa0, sa1, sb0, sb1, *, tq, nq):
    key = lax.broadcasted_iota(jnp.int32, (tq, tq), 0)
    qry = lax.broadcasted_iota(jnp.int32, (tq, tq), 1)
    heads = [slice(hh * HEAD_PAD, (hh + 1) * HEAD_PAD) for hh in range(2)]
    bufs = ((sa0, sa1), (sb0, sb1))
    tiles = [(i, j) for i in range(nq) for j in range(i + 1)]

    def scores(n, hh):
        i, j = tiles[n]
        bufs[n % 2][hh][...] = _dot(k_ref[j * tq:(j + 1) * tq, heads[hh]],
                                    qt_ref[heads[hh], i * tq:(i + 1) * tq])

    def consume(n, hh, state):
        i, j = tiles[n]
        m, acc = state
        s = bufs[n % 2][hh][...]
        if j == i:
            s = jnp.where(key <= qry, s, NEG_BIG)
        m_new = jnp.maximum(m, jnp.max(s, axis=0, keepdims=True))
        p = jnp.exp2(s - m_new).astype(BF16)
        acc = jnp.exp2(m - m_new) * acc + _dot(vt_ref[heads[hh], j * tq:(j + 1) * tq], p)
        return m_new, acc

    for hh in range(2):
        scores(0, hh)
    state = None
    for n, (i, j) in enumerate(tiles):
        if j == 0:
            state = [(jnp.full((1, tq), NEG_BIG, F32), jnp.zeros((HEAD_PAD, tq), F32)) for _ in heads]
        for hh in range(2):
            if n + 1 < len(tiles):
                scores(n + 1, hh)
            state[hh] = consume(n, hh, state[hh])
        if j == i:
            for hh, (_, acc) in enumerate(state):
                o_ref[hh * MLA_V:(hh + 1) * MLA_V, i * tq:(i + 1) * tq] = (
                    acc[:MLA_V] / acc[MLA_V:MLA_V + 1]).astype(BF16)


def _attn(qt, k, vt, *, batch, seq, tq=512):
    T = k.shape[0]
    tq = min(tq, seq)
    pairs = MLA_HEADS // 2
    return pl.pallas_call(
        functools.partial(_attn_kernel, tq=tq, nq=seq // tq),
        out_shape=jax.ShapeDtypeStruct((MLA_HEADS * MLA_V, T), BF16),
        grid=(batch, pairs),
        in_specs=[pl.BlockSpec((2 * HEAD_PAD, seq), lambda b, p: (p, b)),
                  pl.BlockSpec((seq, 2 * HEAD_PAD), lambda b, p: (b, p)),
                  pl.BlockSpec((2 * HEAD_PAD, seq), lambda b, p: (p, b))],
        out_specs=pl.BlockSpec((2 * MLA_V, seq), lambda b, p: (p, b)),
        scratch_shapes=[pltpu.VMEM((tq, tq), F32)] * 4,
        compiler_params=_params(("parallel", "parallel")),
        name="mla_attn",
    )(qt, k, vt)


def _merge_kernel(omt_ref, oh_ref, g01_ref, pm_ref, womla_ref, wohg_ref, wout_ref, o_ref):
    D = pm_ref.shape[1]
    y_mla = _dot_tn(omt_ref[...], womla_ref[...])
    y_hg = _dot(oh_ref[...], wohg_ref[...])
    merged = (g01_ref[:, :D].astype(F32) * y_mla + g01_ref[:, D:].astype(F32) * y_hg
              + pm_ref[...].astype(F32))
    o_ref[...] = _dot(merged.astype(BF16), wout_ref[...]).astype(BF16)


def _merge(o_mla_t, o_hg, g01, pm, womla, wohg, wout, layer, *, tm=512):
    T, D = pm.shape
    tm = min(tm, T)
    row = lambda w: pl.BlockSpec((tm, w), lambda i: (i, 0))
    return pl.pallas_call(
        _merge_kernel,
        out_shape=jax.ShapeDtypeStruct((T, D), BF16),
        grid=(T // tm,),
        in_specs=[pl.BlockSpec((o_mla_t.shape[0], tm), lambda i: (0, i)), row(o_hg.shape[1]),
                  row(2 * D), row(D), _layer_spec(womla, layer), _layer_spec(wohg, layer),
                  _layer_spec(wout, layer)],
        out_specs=row(D),
        compiler_params=_params(("parallel",)),
        name="merge",
    )(o_mla_t, o_hg, g01, pm, womla, wohg, wout)


def _prep_weights(w_in, w_uq, w_uk, w_uv, w_mem_kv):
    L, D, _ = w_in.shape
    offs = np.cumsum([0, Q_LORA, KV_LORA, MLA_ROPE, HG_W, HG_W, HG_W, HG_W, MEM_W, N_BRANCH * D])
    wt = jnp.swapaxes(w_in, 1, 2)
    kr = wt[:, offs[2]:offs[3]]
    kr_group = jnp.concatenate([jnp.zeros((L, MLA_NOPE, D), F32), kr,
                                jnp.zeros((L, HEAD_PAD - MLA_QK, D), F32)], axis=1)
    wa = jnp.concatenate([wt[:, :offs[2]], kr_group], axis=1)
    wh = wt[:, offs[3]:offs[7]]
    wm = wt[:, offs[7]:offs[8]] * (MEM_HDIM ** -0.5)
    wg = wt[:, offs[8]:offs[9]]
    uq = (w_uq * (MLA_QK ** -0.5 * LOG2E)).reshape(L, Q_LORA, MLA_HEADS, MLA_QK)
    uq = jnp.pad(uq, ((0, 0), (0, 0), (0, 0), (0, HEAD_PAD - MLA_QK)))
    uk = w_uk.reshape(L, KV_LORA, MLA_HEADS, MLA_NOPE)
    uk = jnp.pad(uk, ((0, 0), (0, 0), (0, 0), (0, HEAD_PAD - MLA_NOPE)))
    uv = w_uv.reshape(L, KV_LORA, MLA_HEADS, MLA_V)
    uv = jnp.pad(uv, ((0, 0), (0, 0), (0, 0), (0, HEAD_PAD - MLA_V)))
    G = MLA_HEADS * HEAD_PAD
    wkt = jnp.swapaxes(w_mem_kv[:, :, :MEM_W], 1, 2)
    wv = w_mem_kv[:, :, MEM_W:]
    cast = lambda t: t.astype(BF16)
    return dict(wa=cast(wa), wh=cast(wh), wm=cast(wm), wg=cast(wg),
                wuqt=cast(jnp.swapaxes(uq.reshape(L, Q_LORA, G), 1, 2)),
                wuk=cast(uk.reshape(L, KV_LORA, G)),
                wuvt=cast(jnp.swapaxes(uv.reshape(L, KV_LORA, G), 1, 2)),
                wkt=cast(wkt), wv=cast(wv))


def kernel(x, mem, positions, ffn1_norm, w_ffn1_in, w_ffn1_out, mix_norm, w_in, q_lat_norm, kv_lat_norm,
           w_uq, w_uk, w_uv, w_o_mla, hg_lower_bounds, hg_out_norm, w_o_hg, mem_norm, w_mem_kv, w_o_mem,
           w_out, ffn2_norm, w_ffn2_in, w_ffn2_out, final_norm):
    B, S, D = x.shape
    L = w_in.shape[0]
    T = B * S
    vec = lambda t: t.reshape(L, 1, -1)
    pw = _prep_weights(w_in, w_uq, w_uk, w_uv, w_mem_kv)
    w1a, w1b = w_ffn1_in.astype(BF16), w_ffn1_out.astype(BF16)
    w2a, w2b = w_ffn2_in.astype(BF16), w_ffn2_out.astype(BF16)
    womla, wohg, wout = w_o_mla.astype(BF16), w_o_hg.astype(BF16), w_out.astype(BF16)
    stacked = [vec(mix_norm), pw["wa"], pw["wh"], pw["wm"], pw["wg"], vec(q_lat_norm), vec(kv_lat_norm),
               pw["wuqt"], pw["wuk"], pw["wuvt"], w_o_mem.astype(BF16)]

    rows, cols = _rope_tables(positions)
    lb, onemlb = _lower_bounds(hg_lower_bounds)
    stacked += [vec(lb), vec(onemlb), vec(hg_out_norm)]
    kt_all, vm_all = _memkv(mem, mem_norm, pw["wkt"], pw["wv"])

    xf = x.reshape(T, D)
    for l in range(L):
        xf = _ffn(xf, None, vec(ffn1_norm), w1a, w1b, final_norm, l, final_norm=False)
        qt, k, vt, o_hg, g01, pm = _mixer(xf, stacked, l, rows, cols, kt_all, vm_all, seq=S)
        o_mla_t = _attn(qt, k, vt, batch=B, seq=S)
        delta = _merge(o_mla_t, o_hg, g01, pm, womla, wohg, wout, l)
        xf = _ffn(xf, delta, vec(ffn2_norm), w2a, w2b, final_norm, l, final_norm=(l == L - 1))
    return xf.reshape(B, S, D)
```

```python
import functools
import math

import numpy as np
import jax
import jax.numpy as jnp
from jax import lax
from jax.experimental import pallas as pl
from jax.experimental.pallas import tpu as pltpu

F32 = jnp.float32
BF16 = jnp.bfloat16

MLA_HEADS = 8
MLA_NOPE = 64
MLA_ROPE = 32
MLA_V = 64
Q_LORA = 384
KV_LORA = 256
ROPE_THETA = 10000.0
HG_HEADS = 4
HG_KDIM = 128
HG_VDIM = 128
MEM_HEADS = 4
MEM_HDIM = 128
N_BRANCH = 3
NORM_EPS = 1e-6
MLA_QK = MLA_NOPE + MLA_ROPE
HG_W = HG_HEADS * HG_KDIM
MEM_W = MEM_HEADS * MEM_HDIM

LANES = 128
SUBLANES = 8
MXU_TILE = 256
VMEM_LIMIT_BYTES = 56 * 1024 * 1024

HEAD_PAD = LANES
ROPE_HALF = MLA_ROPE // 2
NEG_BIG = -1e30
LOG2E = math.log2(math.e)


def _rms(x, g):
    ms = jnp.mean(x * x, axis=-1, keepdims=True)
    return x * lax.rsqrt(ms + NORM_EPS) * g


def _sigmoid(x):
    return 1.0 / (1.0 + jnp.exp(-x))


def _dot(a, b):
    return jnp.dot(a, b, preferred_element_type=F32)


def _dot_nt(a, b):
    return lax.dot_general(a, b, (((1,), (1,)), ((), ())), preferred_element_type=F32)


def _dot_tn(a, b):
    return lax.dot_general(a, b, (((0,), (0,)), ((), ())), preferred_element_type=F32)


def _const_spec(shape):
    nd = len(shape)
    return pl.BlockSpec(shape, lambda *_: (0,) * nd, pipeline_mode=pl.Buffered(1))


def _layer_spec(arr, layer):
    nd = arr.ndim - 1
    return pl.BlockSpec((None,) + arr.shape[1:], lambda *_: (layer,) + (0,) * nd,
                        pipeline_mode=pl.Buffered(1))


def _params(sem):
    return pltpu.CompilerParams(dimension_semantics=sem, vmem_limit_bytes=VMEM_LIMIT_BYTES)


def _rope_row_kernel(pos_ref, freq_ref, cf_ref, s1_ref, s2_ref):
    ang = pos_ref[...].astype(F32) * freq_ref[...]
    c = jnp.cos(ang)
    s = jnp.sin(ang)
    lane = lax.broadcasted_iota(jnp.int32, ang.shape, 1)
    lo = (lane >= MLA_NOPE) & (lane < MLA_NOPE + ROPE_HALF)
    hi = (lane >= MLA_NOPE + ROPE_HALF) & (lane < MLA_QK)
    cf_ref[...] = jnp.where(lane < MLA_NOPE, 1.0, jnp.where(lane < MLA_QK, c, 0.0))
    s1_ref[...] = jnp.where(lo, -s, 0.0)
    s2_ref[...] = jnp.where(hi, s, 0.0)


def _rope_col_kernel(pos_ref, freq_ref, cos_ref, sin_ref):
    ang = freq_ref[...] * pos_ref[...].astype(F32)
    cos_ref[...] = jnp.cos(ang)
    sin_ref[...] = jnp.sin(ang)


def _rope_tables(positions):
    T = positions.size
    tm = min(T, 1024)
    inv = ROPE_THETA ** (-jnp.arange(0, MLA_ROPE, 2, dtype=F32) / MLA_ROPE)
    freq = jnp.concatenate([jnp.zeros((MLA_NOPE,), F32), inv, inv,
                            jnp.zeros((LANES - MLA_QK,), F32)]).reshape(1, LANES)
    out = jax.ShapeDtypeStruct((T, LANES), F32)
    row = pl.BlockSpec((tm, LANES), lambda i: (i, 0))
    rows = pl.pallas_call(
        _rope_row_kernel,
        out_shape=(out, out, out),
        grid=(T // tm,),
        in_specs=[pl.BlockSpec((tm, 1), lambda i: (i, 0)), _const_spec((1, LANES))],
        out_specs=(row, row, row),
        compiler_params=_params(("parallel",)),
        name="rope_rows",
    )(positions.reshape(T, 1), freq)
    outc = jax.ShapeDtypeStruct((ROPE_HALF, T), F32)
    col = pl.BlockSpec((ROPE_HALF, tm), lambda i: (0, i))
    cols = pl.pallas_call(
        _rope_col_kernel,
        out_shape=(outc, outc),
        grid=(T // tm,),
        in_specs=[pl.BlockSpec((1, tm), lambda i: (0, i)), _const_spec((ROPE_HALF, 1))],
        out_specs=(col, col),
        compiler_params=_params(("parallel",)),
        name="rope_cols",
    )(positions.reshape(1, T), inv.reshape(ROPE_HALF, 1))
    return rows, cols


def _lower_bound_kernel(x_ref, lb_ref, onemlb_ref):
    x = x_ref[...]
    depth = x.shape[0]
    e = jnp.exp(x - jnp.max(x, axis=0, keepdims=True))
    p = e / jnp.sum(e, axis=0, keepdims=True)
    acc = jnp.zeros_like(p[0:1])
    for l in range(depth):
        if l > 0:
            acc = acc + p[l:l + 1]
        lb_ref[l:l + 1, :] = acc
        onemlb_ref[l:l + 1, :] = 1.0 - acc


def _lower_bounds(hg_lower_bounds):
    out = jax.ShapeDtypeStruct(hg_lower_bounds.shape, F32)
    return pl.pallas_call(_lower_bound_kernel, out_shape=(out, out),
                          name="hgrn_lower_bounds")(hg_lower_bounds.astype(F32))


def _memkv_kernel(mem_ref, g_ref, wkt_ref, wv_ref, kt_ref, v_ref):
    mn = _rms(mem_ref[...], g_ref[...]).astype(BF16)
    kt_ref[...] = _dot_nt(wkt_ref[...], mn).astype(BF16)
    v_ref[...] = _dot(mn, wv_ref[...]).astype(BF16)


def _memkv(mem, mem_norm, wkt, wv):
    B, M, D = mem.shape
    L = mem_norm.shape[0]
    return pl.pallas_call(
        _memkv_kernel,
        out_shape=(jax.ShapeDtypeStruct((L, B, MEM_W, M), BF16),
                   jax.ShapeDtypeStruct((L, B, M, MEM_W), BF16)),
        grid=(L, B),
        in_specs=[pl.BlockSpec((None, M, D), lambda l, b: (b, 0, 0)),
                  pl.BlockSpec((None, 1, D), lambda l, b: (l, 0, 0)),
                  pl.BlockSpec((None, MEM_W, D), lambda l, b: (l, 0, 0)),
                  pl.BlockSpec((None, D, MEM_W), lambda l, b: (l, 0, 0))],
        out_specs=(pl.BlockSpec((None, None, MEM_W, M), lambda l, b: (l, b, 0, 0)),
                   pl.BlockSpec((None, None, M, MEM_W), lambda l, b: (l, b, 0, 0))),
        compiler_params=_params(("parallel", "parallel")),
        name="mem_kv",
    )(mem, mem_norm.reshape(L, 1, D), wkt, wv)


def _ffn_chunks(d_ff, n_chunks):
    tiles = d_ff // MXU_TILE
    assert tiles * MXU_TILE == d_ff
    bounds = [MXU_TILE * ((tiles * c) // n_chunks) for c in range(n_chunks + 1)]
    return list(zip(bounds[:-1], bounds[1:]))


def _ffn_kernel(x_ref, g_ref, w1_ref, w2_ref, gf_ref, o_ref, *, d_ff, n_chunks, final_norm):
    x = x_ref[...]
    xn = _rms(x, g_ref[...]).astype(BF16)
    acc = None
    for lo, hi in _ffn_chunks(d_ff, n_chunks):
        a = _dot(xn, w1_ref[:, lo:hi])
        b = _dot(xn, w1_ref[:, d_ff + lo:d_ff + hi])
        h = (a * _sigmoid(a) * b).astype(BF16)
        part = _dot(h, w2_ref[lo:hi, :])
        acc = part if acc is None else acc + part
    y = x + 0.5 * acc
    if final_norm:
        y = _rms(y, gf_ref[...])
    o_ref[...] = y


def _ffn(x, g, w1, w2, gf, layer, *, final_norm, tm=512, n_chunks=2):
    T, D = x.shape
    d_ff = w2.shape[1]
    tm = min(tm, T)
    row = pl.BlockSpec((tm, D), lambda i: (i, 0))
    return pl.pallas_call(
        functools.partial(_ffn_kernel, d_ff=d_ff, n_chunks=n_chunks, final_norm=final_norm),
        out_shape=jax.ShapeDtypeStruct((T, D), F32),
        grid=(T // tm,),
        in_specs=[row, _layer_spec(g, layer), _layer_spec(w1, layer), _layer_spec(w2, layer),
                  _const_spec((1, D))],
        out_specs=row,
        compiler_params=_params(("parallel",)),
        name="ffn",
    )(x, g, w1, w2, gf.reshape(1, D))


def _level_map(n):
    t = np.arange(n)[:, None]
    s = np.arange(n)[None, :]
    x = np.bitwise_xor(t, s)
    lvl = np.floor(np.log2(np.maximum(x, 1))).astype(np.int32)
    out = np.where(s < t, lvl, -1)
    out = np.where(s == t, int(np.log2(n)), out)
    return out.astype(np.int32)


def _hgrn_stages(h_scr, r0, tc, lb_ref, onemlb_ref, onorm_ref, tri_ref, lvl_ref, o_ref, st_ref, g_scr):
    half = tc // 2
    nlev = int(np.log2(half))
    W = HG_W
    rs_all = slice(r0, r0 + tc)

    z = h_scr[rs_all, HG_W:2 * HG_W]
    lb = lb_ref[...]
    e = jnp.exp(-jnp.abs(z))
    pos = z >= 0.0
    lden = jnp.log(1.0 + e)
    g = jnp.log(jnp.where(pos, 1.0 + lb * e, lb + e)) - lden
    g = jnp.maximum(g, jnp.minimum(z, 0.0) - lden) * LOG2E
    kk = onemlb_ref[...] * jnp.where(pos, e, 1.0) / (1.0 + e)
    hq = h_scr[rs_all, :HG_W]
    qq = hq * _sigmoid(hq)

    g_hi = g.astype(BF16)
    r1 = g - g_hi.astype(F32)
    g_mid = r1.astype(BF16)
    g_lo = (r1 - g_mid.astype(F32)).astype(BF16)
    tri = tri_ref[...]
    G = _dot(tri, g_hi) + _dot(tri, g_mid) + _dot(tri, g_lo)
    g_scr[...] = G
    g_last = g_scr[tc - 1:tc, :]
    qb = qq.astype(BF16)
    kb = kk.astype(BF16)
    yield

    rows = lax.broadcasted_iota(jnp.int32, G.shape, 0)
    qe = []
    ke = []
    for lv in range(nlev + 1):
        m = 1 << lv
        if lv == 0:
            d = jnp.where((rows & 1) != 0, g, 0.0)
        elif m >= SUBLANES:
            parts = []
            for b in range(tc // (2 * m)):
                lo = b * 2 * m
                ref_row = g_scr[lo + m - 1:lo + m, :]
                parts.append(ref_row - G[lo:lo + m])
                parts.append(G[lo + m:lo + 2 * m] - ref_row)
            d = jnp.concatenate(parts, axis=0)
        else:
            sub = lax.broadcasted_iota(jnp.int32, (SUBLANES, W), 0)
            parts = []
            for b in range(tc // SUBLANES):
                r = None
                for j in reversed(range(SUBLANES // (2 * m))):
                    rr = b * SUBLANES + j * 2 * m + m - 1
                    rowv = jnp.broadcast_to(g_scr[rr:rr + 1, :], (SUBLANES, W))
                    r = rowv if r is None else jnp.where(sub < (j + 1) * 2 * m, rowv, r)
                parts.append(r)
            d = -jnp.abs(G - jnp.concatenate(parts, axis=0))
        ed = jnp.exp2(d.astype(BF16))
        qe.append(qb * ed)
        ke.append((kb * ed).T)
        if lv % 2 == 1:
            yield

    q_in = (qq * jnp.exp2(G)).astype(BF16)
    k_out = (kk * jnp.exp2(g_last - G)).astype(BF16)
    d_last = jnp.exp2(g_last)
    vv = h_scr[rs_all, 2 * HG_W:3 * HG_W].astype(BF16)
    gate = h_scr[rs_all, 3 * HG_W:]
    lvl = lvl_ref[...]
    yield
    zero = jnp.zeros((half, HG_KDIM), BF16)

    kbt = kb.T

    def pair_products(q_rows, kt_cols, psl):
        ka = kt_cols[psl.start:psl.start + HG_KDIM]
        kb2 = kt_cols[psl.start + HG_KDIM:psl.stop]
        kbd = jnp.concatenate([jnp.concatenate([ka, zero], axis=1),
                               jnp.concatenate([zero, kb2], axis=1)], axis=0)
        return _dot(q_rows[:, psl], kbd)

    blocks = []
    for pair in range(HG_HEADS // 2):
        psl = slice(2 * pair * HG_KDIM, (2 * pair + 2) * HG_KDIM)
        diag = []
        for b0 in (0, half):
            rs = slice(b0, b0 + half)
            A = jnp.where(lvl == nlev, pair_products(qb[rs], kbt[:, rs], psl), 0.0)
            for lv in range(nlev):
                A = jnp.where(lvl == lv, pair_products(qe[lv][rs], ke[lv][:, rs], psl), A)
            diag.append(A.astype(BF16))
        cross = pair_products(qe[nlev][half:], ke[nlev][:, :half], psl).astype(BF16)
        blocks.append((diag, cross))
        yield
    for pair, (diag, cross) in enumerate(blocks):
        for hh in range(2):
            h = 2 * pair + hh
            sl = slice(h * HG_KDIM, (h + 1) * HG_KDIM)
            hs = slice(hh * HG_KDIM, (hh + 1) * HG_KDIM)
            st = st_ref[h]
            v = vv[:, sl]
            o_int = _dot_nt(q_in[:, sl], st.astype(BF16))
            o_top = _dot(diag[0][:, hs], v[:half]) + o_int[:half]
            o_bot = _dot(cross[:, hs], v[:half]) + _dot(diag[1][:, hs], v[half:]) + o_int[half:]
            st_ref[h] = st * d_last[:, sl] + _dot_tn(v, k_out[:, sl])
            gt = gate[:, sl]
            sg = gt * _sigmoid(gt)
            o_ref[r0:r0 + half, sl] = (_rms(o_top, onorm_ref[...]) * sg[:half]).astype(BF16)
            o_ref[r0 + half:r0 + tc, sl] = (_rms(o_bot, onorm_ref[...]) * sg[half:]).astype(BF16)
        yield


def _proj_stages(u, wa_ref, wm_ref, wg_ref, qn_ref, kvn_ref, wuqt_ref, wuk_ref, wuvt_ref, wom_ref,
                 cf_ref, s1_ref, s2_ref, cos_ref, sin_ref, kt_ref, vm_ref,
                 qt_ref, k_ref, vt_ref, g01_ref, pm_ref):
    D = u.shape[1]
    mem_heads = [slice(h * MEM_HDIM, (h + 1) * MEM_HDIM) for h in range(MEM_HEADS)]
    za = _dot_nt(u, wa_ref[...])
    mq = _dot_nt(u, wm_ref[...]).astype(BF16)
    cq = _rms(za[:, :Q_LORA], qn_ref[...]).astype(BF16)
    ckv = _rms(za[:, Q_LORA:Q_LORA + KV_LORA], kvn_ref[...]).astype(BF16)
    t = za[:, Q_LORA + KV_LORA:]
    kr = (t * cf_ref[...] + pltpu.roll(t, LANES - ROPE_HALF, axis=1) * s1_ref[...]
          + pltpu.roll(t, ROPE_HALF, axis=1) * s2_ref[...])
    yield

    mem_s = [_dot(mq[:, sl], kt_ref[sl, :]) for sl in mem_heads]
    k = _dot(ckv, wuk_ref[...])
    for h in range(MLA_HEADS):
        sl = slice(h * HEAD_PAD, (h + 1) * HEAD_PAD)
        k_ref[:, sl] = (k[:, sl] + kr).astype(BF16)
    mem_p = [jnp.exp(s - jnp.max(s, axis=-1, keepdims=True)) for s in mem_s]
    yield

    qt = _dot_nt(wuqt_ref[...], cq)
    cos = cos_ref[...]
    sin = sin_ref[...]
    for h in range(MLA_HEADS):
        r0 = h * HEAD_PAD
        t1 = qt[r0 + MLA_NOPE:r0 + MLA_NOPE + ROPE_HALF]
        t2 = qt[r0 + MLA_NOPE + ROPE_HALF:r0 + MLA_QK]
        qt_ref[r0:r0 + MLA_NOPE, :] = qt[r0:r0 + MLA_NOPE].astype(BF16)
        qt_ref[r0 + MLA_NOPE:r0 + MLA_NOPE + ROPE_HALF, :] = (t1 * cos - t2 * sin).astype(BF16)
        qt_ref[r0 + MLA_NOPE + ROPE_HALF:r0 + MLA_QK, :] = (t2 * cos + t1 * sin).astype(BF16)
        qt_ref[r0 + MLA_QK:r0 + HEAD_PAD, :] = jnp.zeros((HEAD_PAD - MLA_QK, qt.shape[1]), BF16)
    yield
    vt = _dot_nt(wuvt_ref[...], ckv)
    rowi = lax.broadcasted_iota(jnp.int32, vt.shape, 0)
    vt_ref[...] = jnp.where((rowi & (HEAD_PAD - 1)) == MLA_V, 1.0, vt).astype(BF16)
    heads = [(_dot(p.astype(BF16), vm_ref[:, sl]) / jnp.sum(p, axis=-1, keepdims=True)).astype(BF16)
             for p, sl in zip(mem_p, mem_heads)]
    yield

    y_mem = None
    for b in range(N_BRANCH):
        gate = _sigmoid(_dot_nt(u, wg_ref[b * D:(b + 1) * D, :]))
        if b == 0:
            y_mem = _dot(jnp.concatenate(heads, axis=1), wom_ref[...])
        if b < N_BRANCH - 1:
            g01_ref[:, b * D:(b + 1) * D] = gate.astype(BF16)
        else:
            pm_ref[...] = (gate * y_mem).astype(BF16)
        yield


def _mixer_kernel(x_ref, g_ref, wa_ref, wh_ref, wm_ref, wg_ref, qn_ref, kvn_ref, wuqt_ref, wuk_ref,
                  wuvt_ref, wom_ref, lb_ref, onemlb_ref, onorm_ref, tri_ref, lvl_ref,
                  cf_ref, s1_ref, s2_ref, cos_ref, sin_ref, kt_ref, vm_ref,
                  qt_ref, k_ref, vt_ref, oh_ref, g01_ref, pm_ref, st_ref, g_scr, h_scr, *, tc, per_seq):
    @pl.when(pl.program_id(0) % per_seq == 0)
    def _():
        st_ref[...] = jnp.zeros_like(st_ref)

    u = _rms(x_ref[...], g_ref[...]).astype(BF16)
    for c in range(4):
        sl = slice(c * HG_W, (c + 1) * HG_W)
        h_scr[:, sl] = _dot_nt(u, wh_ref[sl, :])

    proj = _proj_stages(u, wa_ref, wm_ref, wg_ref, qn_ref, kvn_ref, wuqt_ref, wuk_ref, wuvt_ref, wom_ref,
                        cf_ref, s1_ref, s2_ref, cos_ref, sin_ref, kt_ref, vm_ref,
                        qt_ref, k_ref, vt_ref, g01_ref, pm_ref)
    tm = x_ref.shape[0]
    recur = (stage for r0 in range(0, tm, tc)
             for stage in _hgrn_stages(h_scr, r0, tc, lb_ref, onemlb_ref, onorm_ref, tri_ref, lvl_ref,
                                       oh_ref, st_ref, g_scr))
    live = [recur, recur, proj]
    while live:
        for gen in list(live):
            if gen in live and next(gen, StopIteration) is StopIteration:
                live = [g for g in live if g is not gen]


def _mixer(x, stacked, layer, rows, cols, kt, vm, *, seq, tm=512, tc=256):
    T, D = x.shape
    tm = min(tm, seq)
    tc = min(tc, tm)
    per_seq = seq // tm
    M = vm.shape[2]
    G = MLA_HEADS * HEAD_PAD
    tri = jnp.asarray(np.tril(np.ones((tc, tc), np.float32)), BF16)
    lvl = jnp.asarray(np.tile(_level_map(tc // 2), (1, 2)))
    row = lambda w: pl.BlockSpec((tm, w), lambda i: (i, 0))
    col = lambda h: pl.BlockSpec((h, tm), lambda i: (0, i))
    return pl.pallas_call(
        functools.partial(_mixer_kernel, tc=tc, per_seq=per_seq),
        out_shape=(jax.ShapeDtypeStruct((G, T), BF16),
                   jax.ShapeDtypeStruct((T, G), BF16),
                   jax.ShapeDtypeStruct((G, T), BF16),
                   jax.ShapeDtypeStruct((T, HG_W), BF16),
                   jax.ShapeDtypeStruct((T, 2 * D), BF16),
                   jax.ShapeDtypeStruct((T, D), BF16)),
        grid=(T // tm,),
        in_specs=[row(D)] + [_layer_spec(w, layer) for w in stacked]
                 + [_const_spec((tc, tc)), _const_spec((tc // 2, tc)),
                    row(LANES), row(LANES), row(LANES), col(ROPE_HALF), col(ROPE_HALF),
                    pl.BlockSpec((None, None, MEM_W, M), lambda i: (layer, i // per_seq, 0, 0)),
                    pl.BlockSpec((None, None, M, MEM_W), lambda i: (layer, i // per_seq, 0, 0))],
        out_specs=(col(G), row(G), col(G), row(HG_W), row(2 * D), row(D)),
        scratch_shapes=[pltpu.VMEM((HG_HEADS, HG_VDIM, HG_KDIM), F32), pltpu.VMEM((tc, HG_W), F32),
                        pltpu.VMEM((tm, 4 * HG_W), F32)],
        compiler_params=_params(("arbitrary",)),
        name="mixer_proj_hgrn",
    )(x, *stacked, tri, lvl, *rows, *cols, kt, vm)


def _attn_kernel(qt_ref, k_ref, vt_ref, o_ref, sa0, sa1, sb0, sb1, *, tq, nq):
    key = lax.broadcasted_iota(jnp.int32, (tq, tq), 0)
    qry = lax.broadcasted_iota(jnp.int32, (tq, tq), 1)
    heads = [slice(hh * HEAD_PAD, (hh + 1) * HEAD_PAD) for hh in range(2)]
    bufs = ((sa0, sa1), (sb0, sb1))
    tiles = [(i, j) for i in range(nq) for j in range(i + 1)]

    def scores(n, hh):
        i, j = tiles[n]
        bufs[n % 2][hh][...] = _dot(k_ref[j * tq:(j + 1) * tq, heads[hh]],
                                    qt_ref[heads[hh], i * tq:(i + 1) * tq])

    def consume(n, hh, state):
        i, j = tiles[n]
        m, acc = state
        s = bufs[n % 2][hh][...]
        if j == i:
            s = jnp.where(key <= qry, s, NEG_BIG)
        m_new = jnp.maximum(m, jnp.max(s, axis=0, keepdims=True))
        p = jnp.exp2(s - m_new).astype(BF16)
        acc = jnp.exp2(m - m_new) * acc + _dot(vt_ref[heads[hh], j * tq:(j + 1) * tq], p)
        return m_new, acc

    for hh in range(2):
        scores(0, hh)
    state = None
    for n, (i, j) in enumerate(tiles):
        if j == 0:
            state = [(jnp.full((1, tq), NEG_BIG, F32), jnp.zeros((HEAD_PAD, tq), F32)) for _ in heads]
        for hh in range(2):
            if n + 1 < len(tiles):
                scores(n + 1, hh)
            state[hh] = consume(n, hh, state[hh])
        if j == i:
            for hh, (_, acc) in enumerate(state):
                o_ref[hh * MLA_V:(hh + 1) * MLA_V, i * tq:(i + 1) * tq] = (
                    acc[:MLA_V] / acc[MLA_V:MLA_V + 1]).astype(BF16)


def _attn(qt, k, vt, *, batch, seq, tq=512):
    T = k.shape[0]
    tq = min(tq, seq)
    pairs = MLA_HEADS // 2
    return pl.pallas_call(
        functools.partial(_attn_kernel, tq=tq, nq=seq // tq),
        out_shape=jax.ShapeDtypeStruct((MLA_HEADS * MLA_V, T), BF16),
        grid=(batch, pairs),
        in_specs=[pl.BlockSpec((2 * HEAD_PAD, seq), lambda b, p: (p, b)),
                  pl.BlockSpec((seq, 2 * HEAD_PAD), lambda b, p: (b, p)),
                  pl.BlockSpec((2 * HEAD_PAD, seq), lambda b, p: (p, b))],
        out_specs=pl.BlockSpec((2 * MLA_V, seq), lambda b, p: (p, b)),
        scratch_shapes=[pltpu.VMEM((tq, tq), F32)] * 4,
        compiler_params=_params(("parallel", "parallel")),
        name="mla_attn",
    )(qt, k, vt)


def _merge_kernel(x_ref, omt_ref, oh_ref, g01_ref, pm_ref, womla_ref, wohg_ref, wout_ref, o_ref):
    D = x_ref.shape[1]
    y_mla = _dot_tn(omt_ref[...], womla_ref[...])
    y_hg = _dot(oh_ref[...], wohg_ref[...])
    merged = (g01_ref[:, :D].astype(F32) * y_mla + g01_ref[:, D:].astype(F32) * y_hg
              + pm_ref[...].astype(F32))
    o_ref[...] = x_ref[...] + _dot(merged.astype(BF16), wout_ref[...])


def _merge(x, o_mla_t, o_hg, g01, pm, womla, wohg, wout, layer, *, tm=512):
    T, D = x.shape
    tm = min(tm, T)
    row = lambda w: pl.BlockSpec((tm, w), lambda i: (i, 0))
    return pl.pallas_call(
        _merge_kernel,
        out_shape=jax.ShapeDtypeStruct((T, D), F32),
        grid=(T // tm,),
        in_specs=[row(D), pl.BlockSpec((o_mla_t.shape[0], tm), lambda i: (0, i)), row(o_hg.shape[1]),
                  row(2 * D), row(D), _layer_spec(womla, layer), _layer_spec(wohg, layer),
                  _layer_spec(wout, layer)],
        out_specs=row(D),
        compiler_params=_params(("parallel",)),
        name="merge",
    )(x, o_mla_t, o_hg, g01, pm, womla, wohg, wout)


def _prep_weights(w_in, w_uq, w_uk, w_uv, w_mem_kv):
    L, D, _ = w_in.shape
    offs = np.cumsum([0, Q_LORA, KV_LORA, MLA_ROPE, HG_W, HG_W, HG_W, HG_W, MEM_W, N_BRANCH * D])
    wt = jnp.swapaxes(w_in, 1, 2)
    kr = wt[:, offs[2]:offs[3]]
    kr_group = jnp.concatenate([jnp.zeros((L, MLA_NOPE, D), F32), kr,
                                jnp.zeros((L, HEAD_PAD - MLA_QK, D), F32)], axis=1)
    wa = jnp.concatenate([wt[:, :offs[2]], kr_group], axis=1)
    wh = wt[:, offs[3]:offs[7]]
    wm = wt[:, offs[7]:offs[8]] * (MEM_HDIM ** -0.5)
    wg = wt[:, offs[8]:offs[9]]
    uq = (w_uq * (MLA_QK ** -0.5 * LOG2E)).reshape(L, Q_LORA, MLA_HEADS, MLA_QK)
    uq = jnp.pad(uq, ((0, 0), (0, 0), (0, 0), (0, HEAD_PAD - MLA_QK)))
    uk = w_uk.reshape(L, KV_LORA, MLA_HEADS, MLA_NOPE)
    uk = jnp.pad(uk, ((0, 0), (0, 0), (0, 0), (0, HEAD_PAD - MLA_NOPE)))
    uv = w_uv.reshape(L, KV_LORA, MLA_HEADS, MLA_V)
    uv = jnp.pad(uv, ((0, 0), (0, 0), (0, 0), (0, HEAD_PAD - MLA_V)))
    G = MLA_HEADS * HEAD_PAD
    wkt = jnp.swapaxes(w_mem_kv[:, :, :MEM_W], 1, 2)
    wv = w_mem_kv[:, :, MEM_W:]
    cast = lambda t: t.astype(BF16)
    return dict(wa=cast(wa), wh=cast(wh), wm=cast(wm), wg=cast(wg),
                wuqt=cast(jnp.swapaxes(uq.reshape(L, Q_LORA, G), 1, 2)),
                wuk=cast(uk.reshape(L, KV_LORA, G)),
                wuvt=cast(jnp.swapaxes(uv.reshape(L, KV_LORA, G), 1, 2)),
                wkt=cast(wkt), wv=cast(wv))


def kernel(x, mem, positions, ffn1_norm, w_ffn1_in, w_ffn1_out, mix_norm, w_in, q_lat_norm, kv_lat_norm,
           w_uq, w_uk, w_uv, w_o_mla, hg_lower_bounds, hg_out_norm, w_o_hg, mem_norm, w_mem_kv, w_o_mem,
           w_out, ffn2_norm, w_ffn2_in, w_ffn2_out, final_norm):
    B, S, D = x.shape
    L = w_in.shape[0]
    T = B * S
    vec = lambda t: t.reshape(L, 1, -1)
    pw = _prep_weights(w_in, w_uq, w_uk, w_uv, w_mem_kv)
    w1a, w1b = w_ffn1_in.astype(BF16), w_ffn1_out.astype(BF16)
    w2a, w2b = w_ffn2_in.astype(BF16), w_ffn2_out.astype(BF16)
    womla, wohg, wout = w_o_mla.astype(BF16), w_o_hg.astype(BF16), w_out.astype(BF16)
    stacked = [vec(mix_norm), pw["wa"], pw["wh"], pw["wm"], pw["wg"], vec(q_lat_norm), vec(kv_lat_norm),
               pw["wuqt"], pw["wuk"], pw["wuvt"], w_o_mem.astype(BF16)]

    rows, cols = _rope_tables(positions)
    lb, onemlb = _lower_bounds(hg_lower_bounds)
    stacked += [vec(lb), vec(onemlb), vec(hg_out_norm)]
    kt_all, vm_all = _memkv(mem, mem_norm, pw["wkt"], pw["wv"])

    xf = x.reshape(T, D)
    for l in range(L):
        xf = _ffn(xf, vec(ffn1_norm), w1a, w1b, final_norm, l, final_norm=False)
        qt, k, vt, o_hg, g01, pm = _mixer(xf, stacked, l, rows, cols, kt_all, vm_all, seq=S)
        o_mla_t = _attn(qt, k, vt, batch=B, seq=S)
        xf = _merge(xf, o_mla_t, o_hg, g01, pm, womla, wohg, wout, l)
        xf = _ffn(xf, vec(ffn2_norm), w2a, w2b, final_norm, l, final_norm=(l == L - 1))
    return xf.reshape(B, S, D)
```

```python
import functools
import math

import numpy as np
import jax
import jax.numpy as jnp
from jax import lax
from jax.experimental import pallas as pl
from jax.experimental.pallas import tpu as pltpu

F32 = jnp.float32
BF16 = jnp.bfloat16

MLA_HEADS = 8
MLA_NOPE = 64
MLA_ROPE = 32
MLA_V = 64
Q_LORA = 384
KV_LORA = 256
ROPE_THETA = 10000.0
HG_HEADS = 4
HG_KDIM = 128
HG_VDIM = 128
MEM_HEADS = 4
MEM_HDIM = 128
N_BRANCH = 3
NORM_EPS = 1e-6
MLA_QK = MLA_NOPE + MLA_ROPE
HG_W = HG_HEADS * HG_KDIM
MEM_W = MEM_HEADS * MEM_HDIM
IN_OFFS = tuple(int(v) for v in np.cumsum([0, Q_LORA, KV_LORA, MLA_ROPE, HG_W, HG_W, HG_W, HG_W, MEM_W]))

LANES = 128
SUBLANES = 8
MXU_TILE = 256
VMEM_LIMIT_BYTES = 56 * 1024 * 1024

HEAD_PAD = LANES
ROPE_HALF = MLA_ROPE // 2
NEG_BIG = -1e30
LOG2E = math.log2(math.e)


def _rms(x, g):
    ms = jnp.mean(x * x, axis=-1, keepdims=True)
    return x * lax.rsqrt(ms + NORM_EPS) * g


def _sigmoid(x):
    return 1.0 / (1.0 + jnp.exp(-x))


def _dot(a, b):
    return jnp.dot(a, b, preferred_element_type=F32)


def _dot_nt(a, b):
    return lax.dot_general(a, b, (((1,), (1,)), ((), ())), preferred_element_type=F32)


def _dot_tn(a, b):
    return lax.dot_general(a, b, (((0,), (0,)), ((), ())), preferred_element_type=F32)


def _const_spec(shape):
    nd = len(shape)
    return pl.BlockSpec(shape, lambda *_: (0,) * nd, pipeline_mode=pl.Buffered(1))


def _layer_spec(arr, layer):
    nd = arr.ndim - 1
    return pl.BlockSpec((None,) + arr.shape[1:], lambda *_: (layer,) + (0,) * nd,
                        pipeline_mode=pl.Buffered(1))


def _params(sem):
    return pltpu.CompilerParams(dimension_semantics=sem, vmem_limit_bytes=VMEM_LIMIT_BYTES)


def _rope_row_kernel(pos_ref, freq_ref, cf_ref, s1_ref, s2_ref):
    ang = pos_ref[...].astype(F32) * freq_ref[...]
    c = jnp.cos(ang)
    s = jnp.sin(ang)
    lane = lax.broadcasted_iota(jnp.int32, ang.shape, 1)
    lo = (lane >= MLA_NOPE) & (lane < MLA_NOPE + ROPE_HALF)
    hi = (lane >= MLA_NOPE + ROPE_HALF) & (lane < MLA_QK)
    cf_ref[...] = jnp.where(lane < MLA_NOPE, 1.0, jnp.where(lane < MLA_QK, c, 0.0))
    s1_ref[...] = jnp.where(lo, -s, 0.0)
    s2_ref[...] = jnp.where(hi, s, 0.0)


def _rope_col_kernel(pos_ref, freq_ref, cos_ref, sin_ref):
    ang = freq_ref[...] * pos_ref[...].astype(F32)
    cos_ref[...] = jnp.cos(ang)
    sin_ref[...] = jnp.sin(ang)


def _rope_tables(positions):
    T = positions.size
    tm = min(T, 1024)
    inv = ROPE_THETA ** (-jnp.arange(0, MLA_ROPE, 2, dtype=F32) / MLA_ROPE)
    freq = jnp.concatenate([jnp.zeros((MLA_NOPE,), F32), inv, inv,
                            jnp.zeros((LANES - MLA_QK,), F32)]).reshape(1, LANES)
    out = jax.ShapeDtypeStruct((T, LANES), F32)
    row = pl.BlockSpec((tm, LANES), lambda i: (i, 0))
    rows = pl.pallas_call(
        _rope_row_kernel,
        out_shape=(out, out, out),
        grid=(T // tm,),
        in_specs=[pl.BlockSpec((tm, 1), lambda i: (i, 0)), _const_spec((1, LANES))],
        out_specs=(row, row, row),
        compiler_params=_params(("parallel",)),
        name="rope_rows",
    )(positions.reshape(T, 1), freq)
    outc = jax.ShapeDtypeStruct((ROPE_HALF, T), F32)
    col = pl.BlockSpec((ROPE_HALF, tm), lambda i: (0, i))
    cols = pl.pallas_call(
        _rope_col_kernel,
        out_shape=(outc, outc),
        grid=(T // tm,),
        in_specs=[pl.BlockSpec((1, tm), lambda i: (0, i)), _const_spec((ROPE_HALF, 1))],
        out_specs=(col, col),
        compiler_params=_params(("parallel",)),
        name="rope_cols",
    )(positions.reshape(1, T), inv.reshape(ROPE_HALF, 1))
    return rows, cols


def _lower_bound_kernel(x_ref, lb_ref, onemlb_ref):
    x = x_ref[...]
    depth = x.shape[0]
    e = jnp.exp(x - jnp.max(x, axis=0, keepdims=True))
    p = e / jnp.sum(e, axis=0, keepdims=True)
    acc = jnp.zeros_like(p[0:1])
    for l in range(depth):
        if l > 0:
            acc = acc + p[l:l + 1]
        lb_ref[l:l + 1, :] = acc
        onemlb_ref[l:l + 1, :] = 1.0 - acc


def _lower_bounds(hg_lower_bounds):
    out = jax.ShapeDtypeStruct(hg_lower_bounds.shape, F32)
    return pl.pallas_call(_lower_bound_kernel, out_shape=(out, out),
                          name="hgrn_lower_bounds")(hg_lower_bounds.astype(F32))


def _memkv_kernel(mem_ref, g_ref, wkt_ref, wv_ref, kt_ref, v_ref):
    mn = _rms(mem_ref[...], g_ref[...]).astype(BF16)
    kt_ref[...] = _dot_nt(wkt_ref[...], mn).astype(BF16)
    v_ref[...] = _dot(mn, wv_ref[...]).astype(BF16)


def _memkv(mem, mem_norm, wkt, wv):
    B, M, D = mem.shape
    L = mem_norm.shape[0]
    return pl.pallas_call(
        _memkv_kernel,
        out_shape=(jax.ShapeDtypeStruct((L, B, MEM_W, M), BF16),
                   jax.ShapeDtypeStruct((L, B, M, MEM_W), BF16)),
        grid=(L, B),
        in_specs=[pl.BlockSpec((None, M, D), lambda l, b: (b, 0, 0)),
                  pl.BlockSpec((None, 1, D), lambda l, b: (l, 0, 0)),
                  pl.BlockSpec((None, MEM_W, D), lambda l, b: (l, 0, 0)),
                  pl.BlockSpec((None, D, MEM_W), lambda l, b: (l, 0, 0))],
        out_specs=(pl.BlockSpec((None, None, MEM_W, M), lambda l, b: (l, b, 0, 0)),
                   pl.BlockSpec((None, None, M, MEM_W), lambda l, b: (l, b, 0, 0))),
        compiler_params=_params(("parallel", "parallel")),
        name="mem_kv",
    )(mem, mem_norm.reshape(L, 1, D), wkt, wv)


def _ffn_chunks(d_ff, n_chunks):
    tiles = d_ff // MXU_TILE
    assert tiles * MXU_TILE == d_ff
    bounds = [MXU_TILE * ((tiles * c) // n_chunks) for c in range(n_chunks + 1)]
    return list(zip(bounds[:-1], bounds[1:]))


def _ffn_kernel(x_ref, g_ref, w1_ref, w2_ref, gf_ref, o_ref, *, d_ff, n_chunks, final_norm):
    x = x_ref[...]
    xn = _rms(x, g_ref[...]).astype(BF16)
    acc = None
    for lo, hi in _ffn_chunks(d_ff, n_chunks):
        a = _dot(xn, w1_ref[:, lo:hi])
        b = _dot(xn, w1_ref[:, d_ff + lo:d_ff + hi])
        h = (a * _sigmoid(a) * b).astype(BF16)
        part = _dot(h, w2_ref[lo:hi, :])
        acc = part if acc is None else acc + part
    y = x + 0.5 * acc
    if final_norm:
        y = _rms(y, gf_ref[...])
    o_ref[...] = y


def _ffn(x, g, w1, w2, gf, layer, *, final_norm, tm=512, n_chunks=2):
    T, D = x.shape
    d_ff = w2.shape[1]
    tm = min(tm, T)
    row = pl.BlockSpec((tm, D), lambda i: (i, 0))
    return pl.pallas_call(
        functools.partial(_ffn_kernel, d_ff=d_ff, n_chunks=n_chunks, final_norm=final_norm),
        out_shape=jax.ShapeDtypeStruct((T, D), F32),
        grid=(T // tm,),
        in_specs=[row, _layer_spec(g, layer), _layer_spec(w1, layer), _layer_spec(w2, layer),
                  _const_spec((1, D))],
        out_specs=row,
        compiler_params=_params(("parallel",)),
        name="ffn",
    )(x, g, w1, w2, gf.reshape(1, D))


def _level_map(n):
    t = np.arange(n)[:, None]
    s = np.arange(n)[None, :]
    x = np.bitwise_xor(t, s)
    lvl = np.floor(np.log2(np.maximum(x, 1))).astype(np.int32)
    out = np.where(s < t, lvl, -1)
    out = np.where(s == t, int(np.log2(n)), out)
    return out.astype(np.int32)


def _hgrn_stages(h_scr, r0, tc, lb_ref, onemlb_ref, onorm_ref, tri_ref, lvl_ref, o_ref, st_ref, g_scr):
    half = tc // 2
    nlev = int(np.log2(half))
    W = HG_W
    rs_all = slice(r0, r0 + tc)

    z = h_scr[rs_all, HG_W:2 * HG_W]
    lb = lb_ref[...]
    e = jnp.exp(-jnp.abs(z))
    pos = z >= 0.0
    lden = jnp.log(1.0 + e)
    g = jnp.log(jnp.where(pos, 1.0 + lb * e, lb + e)) - lden
    g = jnp.maximum(g, jnp.minimum(z, 0.0) - lden) * LOG2E
    kk = onemlb_ref[...] * jnp.where(pos, e, 1.0) / (1.0 + e)
    hq = h_scr[rs_all, :HG_W]
    qq = hq * _sigmoid(hq)

    g_hi = g.astype(BF16)
    r1 = g - g_hi.astype(F32)
    g_mid = r1.astype(BF16)
    g_lo = (r1 - g_mid.astype(F32)).astype(BF16)
    tri = tri_ref[...]
    G = _dot(tri, g_hi) + _dot(tri, g_mid) + _dot(tri, g_lo)
    g_scr[...] = G
    g_last = g_scr[tc - 1:tc, :]
    qb = qq.astype(BF16)
    kb = kk.astype(BF16)
    yield

    rows = lax.broadcasted_iota(jnp.int32, G.shape, 0)
    qe = []
    ke = []
    for lv in range(nlev + 1):
        m = 1 << lv
        if lv == 0:
            d = jnp.where((rows & 1) != 0, g, 0.0)
        elif m >= SUBLANES:
            parts = []
            for b in range(tc // (2 * m)):
                lo = b * 2 * m
                ref_row = g_scr[lo + m - 1:lo + m, :]
                parts.append(ref_row - G[lo:lo + m])
                parts.append(G[lo + m:lo + 2 * m] - ref_row)
            d = jnp.concatenate(parts, axis=0)
        else:
            sub = lax.broadcasted_iota(jnp.int32, (SUBLANES, W), 0)
            parts = []
            for b in range(tc // SUBLANES):
                r = None
                for j in reversed(range(SUBLANES // (2 * m))):
                    rr = b * SUBLANES + j * 2 * m + m - 1
                    rowv = jnp.broadcast_to(g_scr[rr:rr + 1, :], (SUBLANES, W))
                    r = rowv if r is None else jnp.where(sub < (j + 1) * 2 * m, rowv, r)
                parts.append(r)
            d = -jnp.abs(G - jnp.concatenate(parts, axis=0))
        ed = jnp.exp2(d.astype(BF16))
        qe.append(qb * ed)
        ke.append((kb * ed).T)
        if lv % 2 == 1:
            yield

    q_in = (qq * jnp.exp2(G)).astype(BF16)
    k_out = (kk * jnp.exp2(g_last - G)).astype(BF16)
    d_last = jnp.exp2(g_last)
    vv = h_scr[rs_all, 2 * HG_W:3 * HG_W].astype(BF16)
    gate = h_scr[rs_all, 3 * HG_W:]
    lvl = lvl_ref[...]
    yield
    zero = jnp.zeros((half, HG_KDIM), BF16)

    kbt = kb.T

    def pair_products(q_rows, kt_cols, psl):
        ka = kt_cols[psl.start:psl.start + HG_KDIM]
        kb2 = kt_cols[psl.start + HG_KDIM:psl.stop]
        kbd = jnp.concatenate([jnp.concatenate([ka, zero], axis=1),
                               jnp.concatenate([zero, kb2], axis=1)], axis=0)
        return _dot(q_rows[:, psl], kbd)

    blocks = []
    for pair in range(HG_HEADS // 2):
        psl = slice(2 * pair * HG_KDIM, (2 * pair + 2) * HG_KDIM)
        diag = []
        for b0 in (0, half):
            rs = slice(b0, b0 + half)
            A = jnp.where(lvl == nlev, pair_products(qb[rs], kbt[:, rs], psl), 0.0)
            for lv in range(nlev):
                A = jnp.where(lvl == lv, pair_products(qe[lv][rs], ke[lv][:, rs], psl), A)
            diag.append(A.astype(BF16))
        cross = pair_products(qe[nlev][half:], ke[nlev][:, :half], psl).astype(BF16)
        blocks.append((diag, cross))
        yield
    for pair, (diag, cross) in enumerate(blocks):
        for hh in range(2):
            h = 2 * pair + hh
            sl = slice(h * HG_KDIM, (h + 1) * HG_KDIM)
            hs = slice(hh * HG_KDIM, (hh + 1) * HG_KDIM)
            st = st_ref[h]
            v = vv[:, sl]
            o_int = _dot_nt(q_in[:, sl], st.astype(BF16))
            o_top = _dot(diag[0][:, hs], v[:half]) + o_int[:half]
            o_bot = _dot(cross[:, hs], v[:half]) + _dot(diag[1][:, hs], v[half:]) + o_int[half:]
            st_ref[h] = st * d_last[:, sl] + _dot_tn(v, k_out[:, sl])
            gt = gate[:, sl]
            sg = gt * _sigmoid(gt)
            o_ref[r0:r0 + half, sl] = (_rms(o_top, onorm_ref[...]) * sg[:half]).astype(BF16)
            o_ref[r0 + half:r0 + tc, sl] = (_rms(o_bot, onorm_ref[...]) * sg[half:]).astype(BF16)
        yield


def _proj_stages(u, wa_ref, wm_ref, wg_ref, qn_ref, kvn_ref, wuqt_ref, wuk_ref, wuvt_ref, wom_ref,
                 cf_ref, s1_ref, s2_ref, cos_ref, sin_ref, kt_ref, vm_ref,
                 qt_ref, k_ref, vt_ref, g01_ref, pm_ref):
    D = u.shape[1]
    mem_heads = [slice(h * MEM_HDIM, (h + 1) * MEM_HDIM) for h in range(MEM_HEADS)]
    za = _dot_nt(u, wa_ref[...])
    mq = _dot_nt(u, wm_ref[...]).astype(BF16)
    cq = _rms(za[:, :Q_LORA], qn_ref[...]).astype(BF16)
    ckv = _rms(za[:, Q_LORA:Q_LORA + KV_LORA], kvn_ref[...]).astype(BF16)
    t = za[:, Q_LORA + KV_LORA:]
    kr = (t * cf_ref[...] + pltpu.roll(t, LANES - ROPE_HALF, axis=1) * s1_ref[...]
          + pltpu.roll(t, ROPE_HALF, axis=1) * s2_ref[...])
    yield

    mem_s = [_dot(mq[:, sl], kt_ref[sl, :]) for sl in mem_heads]
    k = _dot(ckv, wuk_ref[...])
    for h in range(MLA_HEADS):
        sl = slice(h * HEAD_PAD, (h + 1) * HEAD_PAD)
        k_ref[:, sl] = (k[:, sl] + kr).astype(BF16)
    mem_p = [jnp.exp(s - jnp.max(s, axis=-1, keepdims=True)) for s in mem_s]
    yield

    qt = _dot_nt(wuqt_ref[...], cq)
    cos = cos_ref[...]
    sin = sin_ref[...]
    for h in range(MLA_HEADS):
        r0 = h * HEAD_PAD
        t1 = qt[r0 + MLA_NOPE:r0 + MLA_NOPE + ROPE_HALF]
        t2 = qt[r0 + MLA_NOPE + ROPE_HALF:r0 + MLA_QK]
        qt_ref[r0:r0 + MLA_NOPE, :] = qt[r0:r0 + MLA_NOPE].astype(BF16)
        qt_ref[r0 + MLA_NOPE:r0 + MLA_NOPE + ROPE_HALF, :] = (t1 * cos - t2 * sin).astype(BF16)
        qt_ref[r0 + MLA_NOPE + ROPE_HALF:r0 + MLA_QK, :] = (t2 * cos + t1 * sin).astype(BF16)
        qt_ref[r0 + MLA_QK:r0 + HEAD_PAD, :] = jnp.zeros((HEAD_PAD - MLA_QK, qt.shape[1]), BF16)
    yield
    vt = _dot_nt(wuvt_ref[...], ckv)
    rowi = lax.broadcasted_iota(jnp.int32, vt.shape, 0)
    vt_ref[...] = jnp.where((rowi & (HEAD_PAD - 1)) == MLA_V, 1.0, vt).astype(BF16)
    heads = [(_dot(p.astype(BF16), vm_ref[:, sl]) / jnp.sum(p, axis=-1, keepdims=True)).astype(BF16)
             for p, sl in zip(mem_p, mem_heads)]
    yield

    y_mem = None
    for b in range(N_BRANCH):
        gate = _sigmoid(_dot_nt(u, wg_ref[b * D:(b + 1) * D, :]))
        if b == 0:
            y_mem = _dot(jnp.concatenate(heads, axis=1), wom_ref[...])
        if b < N_BRANCH - 1:
            g01_ref[:, b * D:(b + 1) * D] = gate.astype(BF16)
        else:
            pm_ref[...] = (gate * y_mem).astype(BF16)
        yield


def _mixer_kernel(x_ref, g_ref, wa_ref, wt_ref, qn_ref, kvn_ref, wuqt_ref, wuk_ref,
                  wuvt_ref, wom_ref, lb_ref, onemlb_ref, onorm_ref, tri_ref, lvl_ref,
                  cf_ref, s1_ref, s2_ref, cos_ref, sin_ref, kt_ref, vm_ref,
                  qt_ref, k_ref, vt_ref, oh_ref, g01_ref, pm_ref, st_ref, g_scr, h_scr, *, tc, per_seq):
    @pl.when(pl.program_id(0) % per_seq == 0)
    def _():
        st_ref[...] = jnp.zeros_like(st_ref)

    wh_ref = wt_ref.at[IN_OFFS[3]:IN_OFFS[7]]
    wm_ref = wt_ref.at[IN_OFFS[7]:IN_OFFS[8]]
    wg_ref = wt_ref.at[IN_OFFS[8]:]

    u = _rms(x_ref[...], g_ref[...]).astype(BF16)
    for c in range(4):
        sl = slice(c * HG_W, (c + 1) * HG_W)
        h_scr[:, sl] = _dot_nt(u, wh_ref[sl, :])

    proj = _proj_stages(u, wa_ref, wm_ref, wg_ref, qn_ref, kvn_ref, wuqt_ref, wuk_ref, wuvt_ref, wom_ref,
                        cf_ref, s1_ref, s2_ref, cos_ref, sin_ref, kt_ref, vm_ref,
                        qt_ref, k_ref, vt_ref, g01_ref, pm_ref)
    tm = x_ref.shape[0]
    recur = (stage for r0 in range(0, tm, tc)
             for stage in _hgrn_stages(h_scr, r0, tc, lb_ref, onemlb_ref, onorm_ref, tri_ref, lvl_ref,
                                       oh_ref, st_ref, g_scr))
    live = [recur, recur, proj]
    while live:
        for gen in list(live):
            if gen in live and next(gen, StopIteration) is StopIteration:
                live = [g for g in live if g is not gen]


def _mixer(x, stacked, layer, rows, cols, kt, vm, *, seq, tm=512, tc=256):
    T, D = x.shape
    tm = min(tm, seq)
    tc = min(tc, tm)
    per_seq = seq // tm
    M = vm.shape[2]
    G = MLA_HEADS * HEAD_PAD
    tri = jnp.asarray(np.tril(np.ones((tc, tc), np.float32)), BF16)
    lvl = jnp.asarray(np.tile(_level_map(tc // 2), (1, 2)))
    row = lambda w: pl.BlockSpec((tm, w), lambda i: (i, 0))
    col = lambda h: pl.BlockSpec((h, tm), lambda i: (0, i))
    return pl.pallas_call(
        functools.partial(_mixer_kernel, tc=tc, per_seq=per_seq),
        out_shape=(jax.ShapeDtypeStruct((G, T), BF16),
                   jax.ShapeDtypeStruct((T, G), BF16),
                   jax.ShapeDtypeStruct((G, T), BF16),
                   jax.ShapeDtypeStruct((T, HG_W), BF16),
                   jax.ShapeDtypeStruct((T, 2 * D), BF16),
                   jax.ShapeDtypeStruct((T, D), BF16)),
        grid=(T // tm,),
        in_specs=[row(D)] + [_layer_spec(w, layer) for w in stacked]
                 + [_const_spec((tc, tc)), _const_spec((tc // 2, tc)),
                    row(LANES), row(LANES), row(LANES), col(ROPE_HALF), col(ROPE_HALF),
                    pl.BlockSpec((None, None, MEM_W, M), lambda i: (layer, i // per_seq, 0, 0)),
                    pl.BlockSpec((None, None, M, MEM_W), lambda i: (layer, i // per_seq, 0, 0))],
        out_specs=(col(G), row(G), col(G), row(HG_W), row(2 * D), row(D)),
        scratch_shapes=[pltpu.VMEM((HG_HEADS, HG_VDIM, HG_KDIM), F32), pltpu.VMEM((tc, HG_W), F32),
                        pltpu.VMEM((tm, 4 * HG_W), F32)],
        compiler_params=_params(("arbitrary",)),
        name="mixer_proj_hgrn",
    )(x, *stacked, tri, lvl, *rows, *cols, kt, vm)


def _attn_kernel(qt_ref, k_ref, vt_ref, o_ref, sa0, sa1, sb0, sb1, *, tq, nq):
    key = lax.broadcasted_iota(jnp.int32, (tq, tq), 0)
    qry = lax.broadcasted_iota(jnp.int32, (tq, tq), 1)
    heads = [slice(hh * HEAD_PAD, (hh + 1) * HEAD_PAD) for hh in range(2)]
    bufs = ((sa0, sa1), (sb0, sb1))
    tiles = [(i, j) for i in range(nq) for j in range(i + 1)]

    def scores(n, hh):
        i, j = tiles[n]
        bufs[n % 2][hh][...] = _dot(k_ref[j * tq:(j + 1) * tq, heads[hh]],
                                    qt_ref[heads[hh], i * tq:(i + 1) * tq])

    def consume(n, hh, state):
        i, j = tiles[n]
        m, acc = state
        s = bufs[n % 2][hh][...]
        if j == i:
            s = jnp.where(key <= qry, s, NEG_BIG)
        m_new = jnp.maximum(m, jnp.max(s, axis=0, keepdims=True))
        p = jnp.exp2(s - m_new).astype(BF16)
        acc = jnp.exp2(m - m_new) * acc + _dot(vt_ref[heads[hh], j * tq:(j + 1) * tq], p)
        return m_new, acc

    for hh in range(2):
        scores(0, hh)
    state = None
    for n, (i, j) in enumerate(tiles):
        if j == 0:
            state = [(jnp.full((1, tq), NEG_BIG, F32), jnp.zeros((HEAD_PAD, tq), F32)) for _ in heads]
        for hh in range(2):
            if n + 1 < len(tiles):
                scores(n + 1, hh)
            state[hh] = consume(n, hh, state[hh])
        if j == i:
            for hh, (_, acc) in enumerate(state):
                o_ref[hh * MLA_V:(hh + 1) * MLA_V, i * tq:(i + 1) * tq] = (
                    acc[:MLA_V] / acc[MLA_V:MLA_V + 1]).astype(BF16)


def _attn(qt, k, vt, *, batch, seq, tq=512):
    T = k.shape[0]
    tq = min(tq, seq)
    pairs = MLA_HEADS // 2
    return pl.pallas_call(
        functools.partial(_attn_kernel, tq=tq, nq=seq // tq),
        out_shape=jax.ShapeDtypeStruct((MLA_HEADS * MLA_V, T), BF16),
        grid=(batch, pairs),
        in_specs=[pl.BlockSpec((2 * HEAD_PAD, seq), lambda b, p: (p, b)),
                  pl.BlockSpec((seq, 2 * HEAD_PAD), lambda b, p: (b, p)),
                  pl.BlockSpec((2 * HEAD_PAD, seq), lambda b, p: (p, b))],
        out_specs=pl.BlockSpec((2 * MLA_V, seq), lambda b, p: (p, b)),
        scratch_shapes=[pltpu.VMEM((tq, tq), F32)] * 4,
        compiler_params=_params(("parallel", "parallel")),
        name="mla_attn",
    )(qt, k, vt)


def _merge_kernel(x_ref, omt_ref, oh_ref, g01_ref, pm_ref, womla_ref, wohg_ref, wout_ref, o_ref):
    D = x_ref.shape[1]
    y_mla = _dot_tn(omt_ref[...], womla_ref[...])
    y_hg = _dot(oh_ref[...], wohg_ref[...])
    merged = (g01_ref[:, :D].astype(F32) * y_mla + g01_ref[:, D:].astype(F32) * y_hg
              + pm_ref[...].astype(F32))
    o_ref[...] = x_ref[...] + _dot(merged.astype(BF16), wout_ref[...])


def _merge(x, o_mla_t, o_hg, g01, pm, womla, wohg, wout, layer, *, tm=512):
    T, D = x.shape
    tm = min(tm, T)
    row = lambda w: pl.BlockSpec((tm, w), lambda i: (i, 0))
    return pl.pallas_call(
        _merge_kernel,
        out_shape=jax.ShapeDtypeStruct((T, D), F32),
        grid=(T // tm,),
        in_specs=[row(D), pl.BlockSpec((o_mla_t.shape[0], tm), lambda i: (0, i)), row(o_hg.shape[1]),
                  row(2 * D), row(D), _layer_spec(womla, layer), _layer_spec(wohg, layer),
                  _layer_spec(wout, layer)],
        out_specs=row(D),
        compiler_params=_params(("parallel",)),
        name="merge",
    )(x, o_mla_t, o_hg, g01, pm, womla, wohg, wout)


def _prep_weights(w_in, w_uq, w_uk, w_uv, w_mem_kv):
    L, D, _ = w_in.shape
    wt = jnp.swapaxes(w_in, 1, 2)
    kr = wt[:, IN_OFFS[2]:IN_OFFS[3]]
    kr_group = jnp.concatenate([jnp.zeros((L, MLA_NOPE, D), F32), kr,
                                jnp.zeros((L, HEAD_PAD - MLA_QK, D), F32)], axis=1)
    wa = jnp.concatenate([wt[:, :IN_OFFS[2]], kr_group], axis=1)
    uq = (w_uq * (MLA_QK ** -0.5 * LOG2E)).reshape(L, Q_LORA, MLA_HEADS, MLA_QK)
    uq = jnp.pad(uq, ((0, 0), (0, 0), (0, 0), (0, HEAD_PAD - MLA_QK)))
    uk = w_uk.reshape(L, KV_LORA, MLA_HEADS, MLA_NOPE)
    uk = jnp.pad(uk, ((0, 0), (0, 0), (0, 0), (0, HEAD_PAD - MLA_NOPE)))
    uv = w_uv.reshape(L, KV_LORA, MLA_HEADS, MLA_V)
    uv = jnp.pad(uv, ((0, 0), (0, 0), (0, 0), (0, HEAD_PAD - MLA_V)))
    G = MLA_HEADS * HEAD_PAD
    wkt = jnp.swapaxes(w_mem_kv[:, :, :MEM_W], 1, 2) * (MEM_HDIM ** -0.5)
    wv = w_mem_kv[:, :, MEM_W:]
    cast = lambda t: t.astype(BF16)
    return dict(wa=cast(wa), wt=cast(wt),
                wuqt=cast(jnp.swapaxes(uq.reshape(L, Q_LORA, G), 1, 2)),
                wuk=cast(uk.reshape(L, KV_LORA, G)),
                wuvt=cast(jnp.swapaxes(uv.reshape(L, KV_LORA, G), 1, 2)),
                wkt=cast(wkt), wv=cast(wv))


def kernel(x, mem, positions, ffn1_norm, w_ffn1_in, w_ffn1_out, mix_norm, w_in, q_lat_norm, kv_lat_norm,
           w_uq, w_uk, w_uv, w_o_mla, hg_lower_bounds, hg_out_norm, w_o_hg, mem_norm, w_mem_kv, w_o_mem,
           w_out, ffn2_norm, w_ffn2_in, w_ffn2_out, final_norm):
    B, S, D = x.shape
    L = w_in.shape[0]
    T = B * S
    vec = lambda t: t.reshape(L, 1, -1)
    pw = _prep_weights(w_in, w_uq, w_uk, w_uv, w_mem_kv)
    w1a, w1b = w_ffn1_in.astype(BF16), w_ffn1_out.astype(BF16)
    w2a, w2b = w_ffn2_in.astype(BF16), w_ffn2_out.astype(BF16)
    womla, wohg, wout = w_o_mla.astype(BF16), w_o_hg.astype(BF16), w_out.astype(BF16)
    stacked = [vec(mix_norm), pw["wa"], pw["wt"], vec(q_lat_norm), vec(kv_lat_norm),
               pw["wuqt"], pw["wuk"], pw["wuvt"], w_o_mem.astype(BF16)]

    rows, cols = _rope_tables(positions)
    lb, onemlb = _lower_bounds(hg_lower_bounds)
    stacked += [vec(lb), vec(onemlb), vec(hg_out_norm)]
    kt_all, vm_all = _memkv(mem, mem_norm, pw["wkt"], pw["wv"])

    xf = x.reshape(T, D)
    for l in range(L):
        xf = _ffn(xf, vec(ffn1_norm), w1a, w1b, final_norm, l, final_norm=False)
        qt, k, vt, o_hg, g01, pm = _mixer(xf, stacked, l, rows, cols, kt_all, vm_all, seq=S)
        o_mla_t = _attn(qt, k, vt, batch=B, seq=S)
        xf = _merge(xf, o_mla_t, o_hg, g01, pm, womla, wohg, wout, l)
        xf = _ffn(xf, vec(ffn2_norm), w2a, w2b, final_norm, l, final_norm=(l == L - 1))
    return xf.reshape(B, S, D)
```

```python
import functools
import math

import numpy as np
import jax
import jax.numpy as jnp
from jax import lax
from jax.experimental import pallas as pl
from jax.experimental.pallas import tpu as pltpu

F32 = jnp.float32
BF16 = jnp.bfloat16

MLA_HEADS = 8
MLA_NOPE = 64
MLA_ROPE = 32
MLA_V = 64
Q_LORA = 384
KV_LORA = 256
ROPE_THETA = 10000.0
HG_HEADS = 4
HG_KDIM = 128
HG_VDIM = 128
MEM_HEADS = 4
MEM_HDIM = 128
N_BRANCH = 3
NORM_EPS = 1e-6
MLA_QK = MLA_NOPE + MLA_ROPE
HG_W = HG_HEADS * HG_KDIM
MEM_W = MEM_HEADS * MEM_HDIM
IN_OFFS = tuple(int(v) for v in np.cumsum([0, Q_LORA, KV_LORA, MLA_ROPE, HG_W, HG_W, HG_W, HG_W, MEM_W]))

LANES = 128
SUBLANES = 8
MXU_TILE = 256
VMEM_LIMIT_BYTES = 56 * 1024 * 1024

HEAD_PAD = LANES
ROPE_HALF = MLA_ROPE // 2
NEG_BIG = -1e30
LOG2E = math.log2(math.e)


def _rms(x, g):
    ms = jnp.mean(x * x, axis=-1, keepdims=True)
    return x * lax.rsqrt(ms + NORM_EPS) * g


def _sigmoid(x):
    return 1.0 / (1.0 + jnp.exp(-x))


def _dot(a, b):
    return jnp.dot(a, b, preferred_element_type=F32)


def _dot_nt(a, b):
    return lax.dot_general(a, b, (((1,), (1,)), ((), ())), preferred_element_type=F32)


def _dot_tn(a, b):
    return lax.dot_general(a, b, (((0,), (0,)), ((), ())), preferred_element_type=F32)


def _const_spec(shape):
    nd = len(shape)
    return pl.BlockSpec(shape, lambda *_: (0,) * nd, pipeline_mode=pl.Buffered(1))


def _layer_spec(arr, layer):
    nd = arr.ndim - 1
    return pl.BlockSpec((None,) + arr.shape[1:], lambda *_: (layer,) + (0,) * nd,
                        pipeline_mode=pl.Buffered(1))


def _params(sem):
    return pltpu.CompilerParams(dimension_semantics=sem, vmem_limit_bytes=VMEM_LIMIT_BYTES)


def _rope_row_kernel(pos_ref, freq_ref, cf_ref, s1_ref, s2_ref):
    ang = pos_ref[...].astype(F32) * freq_ref[...]
    c = jnp.cos(ang)
    s = jnp.sin(ang)
    lane = lax.broadcasted_iota(jnp.int32, ang.shape, 1)
    lo = (lane >= MLA_NOPE) & (lane < MLA_NOPE + ROPE_HALF)
    hi = (lane >= MLA_NOPE + ROPE_HALF) & (lane < MLA_QK)
    cf_ref[...] = jnp.where(lane < MLA_NOPE, 1.0, jnp.where(lane < MLA_QK, c, 0.0))
    s1_ref[...] = jnp.where(lo, -s, 0.0)
    s2_ref[...] = jnp.where(hi, s, 0.0)


def _rope_col_kernel(pos_ref, freq_ref, cos_ref, sin_ref):
    ang = freq_ref[...] * pos_ref[...].astype(F32)
    cos_ref[...] = jnp.cos(ang)
    sin_ref[...] = jnp.sin(ang)


def _rope_tables(positions):
    T = positions.size
    tm = min(T, 1024)
    inv = ROPE_THETA ** (-jnp.arange(0, MLA_ROPE, 2, dtype=F32) / MLA_ROPE)
    freq = jnp.concatenate([jnp.zeros((MLA_NOPE,), F32), inv, inv,
                            jnp.zeros((LANES - MLA_QK,), F32)]).reshape(1, LANES)
    out = jax.ShapeDtypeStruct((T, LANES), F32)
    row = pl.BlockSpec((tm, LANES), lambda i: (i, 0))
    rows = pl.pallas_call(
        _rope_row_kernel,
        out_shape=(out, out, out),
        grid=(T // tm,),
        in_specs=[pl.BlockSpec((tm, 1), lambda i: (i, 0)), _const_spec((1, LANES))],
        out_specs=(row, row, row),
        compiler_params=_params(("parallel",)),
        name="rope_rows",
    )(positions.reshape(T, 1), freq)
    outc = jax.ShapeDtypeStruct((ROPE_HALF, T), F32)
    col = pl.BlockSpec((ROPE_HALF, tm), lambda i: (0, i))
    cols = pl.pallas_call(
        _rope_col_kernel,
        out_shape=(outc, outc),
        grid=(T // tm,),
        in_specs=[pl.BlockSpec((1, tm), lambda i: (0, i)), _const_spec((ROPE_HALF, 1))],
        out_specs=(col, col),
        compiler_params=_params(("parallel",)),
        name="rope_cols",
    )(positions.reshape(1, T), inv.reshape(ROPE_HALF, 1))
    return rows, cols


def _lower_bound_kernel(x_ref, lb_ref, onemlb_ref):
    x = x_ref[...]
    depth = x.shape[0]
    e = jnp.exp(x - jnp.max(x, axis=0, keepdims=True))
    p = e / jnp.sum(e, axis=0, keepdims=True)
    acc = jnp.zeros_like(p[0:1])
    for l in range(depth):
        if l > 0:
            acc = acc + p[l:l + 1]
        lb_ref[l:l + 1, :] = acc
        onemlb_ref[l:l + 1, :] = 1.0 - acc


def _lower_bounds(hg_lower_bounds):
    out = jax.ShapeDtypeStruct(hg_lower_bounds.shape, F32)
    return pl.pallas_call(_lower_bound_kernel, out_shape=(out, out),
                          name="hgrn_lower_bounds")(hg_lower_bounds.astype(F32))


def _memkv_kernel(mem_ref, g_ref, wkt_ref, wv_ref, kt_ref, v_ref):
    mn = _rms(mem_ref[...], g_ref[...]).astype(BF16)
    kt_ref[...] = _dot_nt(wkt_ref[...], mn).astype(BF16)
    v_ref[...] = _dot(mn, wv_ref[...]).astype(BF16)


def _memkv(mem, mem_norm, wkt, wv):
    B, M, D = mem.shape
    L = mem_norm.shape[0]
    return pl.pallas_call(
        _memkv_kernel,
        out_shape=(jax.ShapeDtypeStruct((L, B, MEM_W, M), BF16),
                   jax.ShapeDtypeStruct((L, B, M, MEM_W), BF16)),
        grid=(L, B),
        in_specs=[pl.BlockSpec((None, M, D), lambda l, b: (b, 0, 0)),
                  pl.BlockSpec((None, 1, D), lambda l, b: (l, 0, 0)),
                  pl.BlockSpec((None, MEM_W, D), lambda l, b: (l, 0, 0)),
                  pl.BlockSpec((None, D, MEM_W), lambda l, b: (l, 0, 0))],
        out_specs=(pl.BlockSpec((None, None, MEM_W, M), lambda l, b: (l, b, 0, 0)),
                   pl.BlockSpec((None, None, M, MEM_W), lambda l, b: (l, b, 0, 0))),
        compiler_params=_params(("parallel", "parallel")),
        name="mem_kv",
    )(mem, mem_norm.reshape(L, 1, D), wkt, wv)


def _ffn_chunks(d_ff, n_chunks):
    tiles = d_ff // MXU_TILE
    assert tiles * MXU_TILE == d_ff
    bounds = [MXU_TILE * ((tiles * c) // n_chunks) for c in range(n_chunks + 1)]
    return list(zip(bounds[:-1], bounds[1:]))


def _ffn_kernel(x_ref, g_ref, w1_ref, w2_ref, gf_ref, o_ref, *, d_ff, n_chunks, final_norm):
    x = x_ref[...]
    xn = _rms(x, g_ref[...]).astype(BF16)
    acc = None
    for lo, hi in _ffn_chunks(d_ff, n_chunks):
        a = _dot(xn, w1_ref[:, lo:hi])
        b = _dot(xn, w1_ref[:, d_ff + lo:d_ff + hi])
        h = (a * _sigmoid(a) * b).astype(BF16)
        part = _dot(h, w2_ref[lo:hi, :])
        acc = part if acc is None else acc + part
    y = x + 0.5 * acc
    if final_norm:
        y = _rms(y, gf_ref[...])
    o_ref[...] = y


def _ffn(x, g, w1, w2, gf, layer, *, final_norm, tm=512, n_chunks=2):
    T, D = x.shape
    d_ff = w2.shape[1]
    tm = min(tm, T)
    row = pl.BlockSpec((tm, D), lambda i: (i, 0))
    return pl.pallas_call(
        functools.partial(_ffn_kernel, d_ff=d_ff, n_chunks=n_chunks, final_norm=final_norm),
        out_shape=jax.ShapeDtypeStruct((T, D), F32),
        grid=(T // tm,),
        in_specs=[row, _layer_spec(g, layer), _layer_spec(w1, layer), _layer_spec(w2, layer),
                  _const_spec((1, D))],
        out_specs=row,
        compiler_params=_params(("parallel",)),
        name="ffn",
    )(x, g, w1, w2, gf.reshape(1, D))


def _level_map(n):
    t = np.arange(n)[:, None]
    s = np.arange(n)[None, :]
    x = np.bitwise_xor(t, s)
    lvl = np.floor(np.log2(np.maximum(x, 1))).astype(np.int32)
    out = np.where(s < t, lvl, -1)
    out = np.where(s == t, int(np.log2(n)), out)
    return out.astype(np.int32)


def _hgrn_stages(h_scr, r0, tc, lb_ref, onemlb_ref, onorm_ref, tri_ref, lvl_ref, o_ref, st_ref, g_scr):
    half = tc // 2
    nlev = int(np.log2(half))
    W = HG_W
    rs_all = slice(r0, r0 + tc)

    z = h_scr[rs_all, HG_W:2 * HG_W]
    lb = lb_ref[...]
    e = jnp.exp(-jnp.abs(z))
    pos = z >= 0.0
    lden = jnp.log(1.0 + e)
    g = jnp.log(jnp.where(pos, 1.0 + lb * e, lb + e)) - lden
    g = jnp.maximum(g, jnp.minimum(z, 0.0) - lden) * LOG2E
    kk = onemlb_ref[...] * jnp.where(pos, e, 1.0) / (1.0 + e)
    hq = h_scr[rs_all, :HG_W]
    qq = hq * _sigmoid(hq)

    g_hi = g.astype(BF16)
    r1 = g - g_hi.astype(F32)
    g_mid = r1.astype(BF16)
    g_lo = (r1 - g_mid.astype(F32)).astype(BF16)
    tri = tri_ref[...]
    G = _dot(tri, g_hi) + _dot(tri, g_mid) + _dot(tri, g_lo)
    g_scr[...] = G
    g_last = g_scr[tc - 1:tc, :]
    qb = qq.astype(BF16)
    kb = kk.astype(BF16)
    yield

    rows = lax.broadcasted_iota(jnp.int32, G.shape, 0)
    qe = []
    ke = []
    for lv in range(nlev + 1):
        m = 1 << lv
        if lv == 0:
            d = jnp.where((rows & 1) != 0, g, 0.0)
        elif m >= SUBLANES:
            parts = []
            for b in range(tc // (2 * m)):
                lo = b * 2 * m
                ref_row = g_scr[lo + m - 1:lo + m, :]
                parts.append(ref_row - G[lo:lo + m])
                parts.append(G[lo + m:lo + 2 * m] - ref_row)
            d = jnp.concatenate(parts, axis=0)
        else:
            sub = lax.broadcasted_iota(jnp.int32, (SUBLANES, W), 0)
            parts = []
            for b in range(tc // SUBLANES):
                r = None
                for j in reversed(range(SUBLANES // (2 * m))):
                    rr = b * SUBLANES + j * 2 * m + m - 1
                    rowv = jnp.broadcast_to(g_scr[rr:rr + 1, :], (SUBLANES, W))
                    r = rowv if r is None else jnp.where(sub < (j + 1) * 2 * m, rowv, r)
                parts.append(r)
            d = -jnp.abs(G - jnp.concatenate(parts, axis=0))
        ed = jnp.exp2(d.astype(BF16))
        qe.append(qb * ed)
        ke.append((kb * ed).T)
        if lv % 2 == 1:
            yield

    q_in = (qq * jnp.exp2(G)).astype(BF16)
    k_out = (kk * jnp.exp2(g_last - G)).astype(BF16)
    d_last = jnp.exp2(g_last)
    vv = h_scr[rs_all, 2 * HG_W:3 * HG_W].astype(BF16)
    gate = h_scr[rs_all, 3 * HG_W:]
    lvl = lvl_ref[...]
    yield
    zero = jnp.zeros((half, HG_KDIM), BF16)

    kbt = kb.T

    def pair_products(q_rows, kt_cols, psl):
        ka = kt_cols[psl.start:psl.start + HG_KDIM]
        kb2 = kt_cols[psl.start + HG_KDIM:psl.stop]
        kbd = jnp.concatenate([jnp.concatenate([ka, zero], axis=1),
                               jnp.concatenate([zero, kb2], axis=1)], axis=0)
        return _dot(q_rows[:, psl], kbd)

    blocks = []
    for pair in range(HG_HEADS // 2):
        psl = slice(2 * pair * HG_KDIM, (2 * pair + 2) * HG_KDIM)
        diag = []
        for b0 in (0, half):
            rs = slice(b0, b0 + half)
            A = jnp.where(lvl == nlev, pair_products(qb[rs], kbt[:, rs], psl), 0.0)
            for lv in range(nlev):
                A = jnp.where(lvl == lv, pair_products(qe[lv][rs], ke[lv][:, rs], psl), A)
            diag.append(A.astype(BF16))
        cross = pair_products(qe[nlev][half:], ke[nlev][:, :half], psl).astype(BF16)
        blocks.append((diag, cross))
        yield
    for pair, (diag, cross) in enumerate(blocks):
        for hh in range(2):
            h = 2 * pair + hh
            sl = slice(h * HG_KDIM, (h + 1) * HG_KDIM)
            hs = slice(hh * HG_KDIM, (hh + 1) * HG_KDIM)
            st = st_ref[h]
            v = vv[:, sl]
            o_int = _dot_nt(q_in[:, sl], st.astype(BF16))
            o_top = _dot(diag[0][:, hs], v[:half]) + o_int[:half]
            o_bot = _dot(cross[:, hs], v[:half]) + _dot(diag[1][:, hs], v[half:]) + o_int[half:]
            st_ref[h] = st * d_last[:, sl] + _dot_tn(v, k_out[:, sl])
            gt = gate[:, sl]
            sg = gt * _sigmoid(gt)
            o_ref[r0:r0 + half, sl] = (_rms(o_top, onorm_ref[...]) * sg[:half]).astype(BF16)
            o_ref[r0 + half:r0 + tc, sl] = (_rms(o_bot, onorm_ref[...]) * sg[half:]).astype(BF16)
        yield


def _proj_stages(u, wa_ref, wm_ref, wg_ref, qn_ref, kvn_ref, wuqt_ref, wuk_ref, wuvt_ref, wom_ref,
                 cf_ref, s1_ref, s2_ref, cos_ref, sin_ref, kt_ref, vm_ref,
                 qt_ref, k_ref, vt_ref, g01_ref, pm_ref):
    D = u.shape[1]
    mem_heads = [slice(h * MEM_HDIM, (h + 1) * MEM_HDIM) for h in range(MEM_HEADS)]
    za = _dot_nt(u, wa_ref[...])
    mq = _dot_nt(u, wm_ref[...]).astype(BF16)
    cq = _rms(za[:, :Q_LORA], qn_ref[...]).astype(BF16)
    ckv = _rms(za[:, Q_LORA:Q_LORA + KV_LORA], kvn_ref[...]).astype(BF16)
    t = za[:, Q_LORA + KV_LORA:]
    kr = (t * cf_ref[...] + pltpu.roll(t, LANES - ROPE_HALF, axis=1) * s1_ref[...]
          + pltpu.roll(t, ROPE_HALF, axis=1) * s2_ref[...])
    yield

    mem_s = [_dot(mq[:, sl], kt_ref[sl, :]) for sl in mem_heads]
    k = _dot(ckv, wuk_ref[...])
    for h in range(MLA_HEADS):
        sl = slice(h * HEAD_PAD, (h + 1) * HEAD_PAD)
        k_ref[:, sl] = (k[:, sl] + kr).astype(BF16)
    mem_p = [jnp.exp(s - jnp.max(s, axis=-1, keepdims=True)) for s in mem_s]
    yield

    qt = _dot_nt(wuqt_ref[...], cq)
    cos = cos_ref[...]
    sin = sin_ref[...]
    for h in range(MLA_HEADS):
        r0 = h * HEAD_PAD
        t1 = qt[r0 + MLA_NOPE:r0 + MLA_NOPE + ROPE_HALF]
        t2 = qt[r0 + MLA_NOPE + ROPE_HALF:r0 + MLA_QK]
        qt_ref[r0:r0 + MLA_NOPE, :] = qt[r0:r0 + MLA_NOPE].astype(BF16)
        qt_ref[r0 + MLA_NOPE:r0 + MLA_NOPE + ROPE_HALF, :] = (t1 * cos - t2 * sin).astype(BF16)
        qt_ref[r0 + MLA_NOPE + ROPE_HALF:r0 + MLA_QK, :] = (t2 * cos + t1 * sin).astype(BF16)
        qt_ref[r0 + MLA_QK:r0 + HEAD_PAD, :] = jnp.zeros((HEAD_PAD - MLA_QK, qt.shape[1]), BF16)
    yield
    vt = _dot_nt(wuvt_ref[...], ckv)
    rowi = lax.broadcasted_iota(jnp.int32, vt.shape, 0)
    vt_ref[...] = jnp.where((rowi & (HEAD_PAD - 1)) == MLA_V, 1.0, vt).astype(BF16)
    heads = [(_dot(p.astype(BF16), vm_ref[:, sl]) / jnp.sum(p, axis=-1, keepdims=True)).astype(BF16)
             for p, sl in zip(mem_p, mem_heads)]
    yield

    y_mem = None
    for b in range(N_BRANCH):
        gate = _sigmoid(_dot_nt(u, wg_ref[b * D:(b + 1) * D, :]))
        if b == 0:
            y_mem = _dot(jnp.concatenate(heads, axis=1), wom_ref[...])
        if b < N_BRANCH - 1:
            g01_ref[:, b * D:(b + 1) * D] = gate.astype(BF16)
        else:
            pm_ref[...] = (gate * y_mem).astype(BF16)
        yield


def _mixer_kernel(x_ref, g_ref, wa_ref, wt_ref, qn_ref, kvn_ref, wuqt_ref, wuk_ref,
                  wuvt_ref, wom_ref, lb_ref, onemlb_ref, onorm_ref, tri_ref, lvl_ref,
                  cf_ref, s1_ref, s2_ref, cos_ref, sin_ref, kt_ref, vm_ref,
                  qt_ref, k_ref, vt_ref, oh_ref, g01_ref, pm_ref, st_ref, g_scr, h_scr, *, tc, per_seq):
    @pl.when(pl.program_id(0) % per_seq == 0)
    def _():
        st_ref[...] = jnp.zeros_like(st_ref)

    wh_ref = wt_ref.at[IN_OFFS[3]:IN_OFFS[7]]
    wm_ref = wt_ref.at[IN_OFFS[7]:IN_OFFS[8]]
    wg_ref = wt_ref.at[IN_OFFS[8]:]

    u = _rms(x_ref[...], g_ref[...]).astype(BF16)
    for c in range(4):
        sl = slice(c * HG_W, (c + 1) * HG_W)
        h_scr[:, sl] = _dot_nt(u, wh_ref[sl, :])

    proj = _proj_stages(u, wa_ref, wm_ref, wg_ref, qn_ref, kvn_ref, wuqt_ref, wuk_ref, wuvt_ref, wom_ref,
                        cf_ref, s1_ref, s2_ref, cos_ref, sin_ref, kt_ref, vm_ref,
                        qt_ref, k_ref, vt_ref, g01_ref, pm_ref)
    tm = x_ref.shape[0]
    recur = (stage for r0 in range(0, tm, tc)
             for stage in _hgrn_stages(h_scr, r0, tc, lb_ref, onemlb_ref, onorm_ref, tri_ref, lvl_ref,
                                       oh_ref, st_ref, g_scr))
    live = [recur, recur, proj]
    while live:
        for gen in list(live):
            if gen in live and next(gen, StopIteration) is StopIteration:
                live = [g for g in live if g is not gen]


def _mixer(x, stacked, layer, rows, cols, kt, vm, *, seq, tm=512, tc=256):
    T, D = x.shape
    tm = min(tm, seq)
    tc = min(tc, tm)
    per_seq = seq // tm
    M = vm.shape[2]
    G = MLA_HEADS * HEAD_PAD
    tri = jnp.asarray(np.tril(np.ones((tc, tc), np.float32)), BF16)
    lvl = jnp.asarray(np.tile(_level_map(tc // 2), (1, 2)))
    row = lambda w: pl.BlockSpec((tm, w), lambda i: (i, 0))
    col = lambda h: pl.BlockSpec((h, tm), lambda i: (0, i))
    return pl.pallas_call(
        functools.partial(_mixer_kernel, tc=tc, per_seq=per_seq),
        out_shape=(jax.ShapeDtypeStruct((G, T), BF16),
                   jax.ShapeDtypeStruct((T, G), BF16),
                   jax.ShapeDtypeStruct((G, T), BF16),
                   jax.ShapeDtypeStruct((T, HG_W), BF16),
                   jax.ShapeDtypeStruct((T, 2 * D), BF16),
                   jax.ShapeDtypeStruct((T, D), BF16)),
        grid=(T // tm,),
        in_specs=[row(D)] + [_layer_spec(w, layer) for w in stacked]
                 + [_const_spec((tc, tc)), _const_spec((tc // 2, tc)),
                    row(LANES), row(LANES), row(LANES), col(ROPE_HALF), col(ROPE_HALF),
                    pl.BlockSpec((None, None, MEM_W, M), lambda i: (layer, i // per_seq, 0, 0)),
                    pl.BlockSpec((None, None, M, MEM_W), lambda i: (layer, i // per_seq, 0, 0))],
        out_specs=(col(G), row(G), col(G), row(HG_W), row(2 * D), row(D)),
        scratch_shapes=[pltpu.VMEM((HG_HEADS, HG_VDIM, HG_KDIM), F32), pltpu.VMEM((tc, HG_W), F32),
                        pltpu.VMEM((tm, 4 * HG_W), F32)],
        compiler_params=_params(("arbitrary",)),
        name="mixer_proj_hgrn",
    )(x, *stacked, tri, lvl, *rows, *cols, kt, vm)


def _attn_kernel(qt_ref, k_ref, vt_ref, o_ref, sa0, sa1, sb0, sb1, *, tq, nq):
    half = tq // 2
    key = lax.broadcasted_iota(jnp.int32, (half, half), 0)
    qry = lax.broadcasted_iota(jnp.int32, (half, half), 1)
    heads = [slice(hh * HEAD_PAD, (hh + 1) * HEAD_PAD) for hh in range(2)]
    bufs = ((sa0, sa1), (sb0, sb1))
    items = []
    for i in range(nq):
        items += [(i, j * tq, tq, ((0, False), (1, False))) for j in range(i)]
        items += [(i, i * tq, half, ((0, True), (1, False))), (i, i * tq + half, half, ((1, True),))]

    def scores(n, hh):
        i, ks, kl, parts = items[n]
        q0 = i * tq + parts[0][0] * half
        bufs[n % 2][hh][:kl, :half * len(parts)] = _dot(k_ref[ks:ks + kl, heads[hh]],
                                                        qt_ref[heads[hh], q0:(i + 1) * tq])

    def consume(n, hh, state):
        i, ks, kl, parts = items[n]
        vt = vt_ref[heads[hh], ks:ks + kl]
        for c, (qh, masked) in enumerate(parts):
            m, acc = state[qh]
            s = bufs[n % 2][hh][:kl, c * half:(c + 1) * half]
            if masked:
                s = jnp.where(key <= qry, s, NEG_BIG)
            m_new = jnp.maximum(m, jnp.max(s, axis=0, keepdims=True))
            p = jnp.exp2(s - m_new).astype(BF16)
            state[qh] = (m_new, jnp.exp2(m - m_new) * acc + _dot(vt, p))

    for hh in range(2):
        scores(0, hh)
    state = None
    for n, (i, ks, _, parts) in enumerate(items):
        if ks == 0:
            state = [[(jnp.full((1, half), NEG_BIG, F32), jnp.zeros((HEAD_PAD, half), F32))
                      for _ in range(2)] for _ in heads]
        for hh in range(2):
            if n + 1 < len(items):
                scores(n + 1, hh)
            consume(n, hh, state[hh])
        if len(parts) == 1:
            for hh in range(2):
                for qh, (_, acc) in enumerate(state[hh]):
                    q0 = i * tq + qh * half
                    o_ref[hh * MLA_V:(hh + 1) * MLA_V, q0:q0 + half] = (
                        acc[:MLA_V] / acc[MLA_V:MLA_V + 1]).astype(BF16)


def _attn(qt, k, vt, *, batch, seq, tq=512):
    T = k.shape[0]
    tq = min(tq, seq)
    pairs = MLA_HEADS // 2
    return pl.pallas_call(
        functools.partial(_attn_kernel, tq=tq, nq=seq // tq),
        out_shape=jax.ShapeDtypeStruct((MLA_HEADS * MLA_V, T), BF16),
        grid=(batch, pairs),
        in_specs=[pl.BlockSpec((2 * HEAD_PAD, seq), lambda b, p: (p, b)),
                  pl.BlockSpec((seq, 2 * HEAD_PAD), lambda b, p: (b, p)),
                  pl.BlockSpec((2 * HEAD_PAD, seq), lambda b, p: (p, b))],
        out_specs=pl.BlockSpec((2 * MLA_V, seq), lambda b, p: (p, b)),
        scratch_shapes=[pltpu.VMEM((tq, tq), F32)] * 4,
        compiler_params=_params(("parallel", "parallel")),
        name="mla_attn",
    )(qt, k, vt)


def _merge_kernel(x_ref, omt_ref, oh_ref, g01_ref, pm_ref, womla_ref, wohg_ref, wout_ref, o_ref):
    D = x_ref.shape[1]
    y_mla = _dot_tn(omt_ref[...], womla_ref[...])
    y_hg = _dot(oh_ref[...], wohg_ref[...])
    merged = (g01_ref[:, :D].astype(F32) * y_mla + g01_ref[:, D:].astype(F32) * y_hg
              + pm_ref[...].astype(F32))
    o_ref[...] = x_ref[...] + _dot(merged.astype(BF16), wout_ref[...])


def _merge(x, o_mla_t, o_hg, g01, pm, womla, wohg, wout, layer, *, tm=512):
    T, D = x.shape
    tm = min(tm, T)
    row = lambda w: pl.BlockSpec((tm, w), lambda i: (i, 0))
    return pl.pallas_call(
        _merge_kernel,
        out_shape=jax.ShapeDtypeStruct((T, D), F32),
        grid=(T // tm,),
        in_specs=[row(D), pl.BlockSpec((o_mla_t.shape[0], tm), lambda i: (0, i)), row(o_hg.shape[1]),
                  row(2 * D), row(D), _layer_spec(womla, layer), _layer_spec(wohg, layer),
                  _layer_spec(wout, layer)],
        out_specs=row(D),
        compiler_params=_params(("parallel",)),
        name="merge",
    )(x, o_mla_t, o_hg, g01, pm, womla, wohg, wout)


def _prep_weights(w_in, w_uq, w_uk, w_uv, w_mem_kv):
    L, D, _ = w_in.shape
    wt = jnp.swapaxes(w_in, 1, 2)
    kr = wt[:, IN_OFFS[2]:IN_OFFS[3]]
    kr_group = jnp.concatenate([jnp.zeros((L, MLA_NOPE, D), F32), kr,
                                jnp.zeros((L, HEAD_PAD - MLA_QK, D), F32)], axis=1)
    wa = jnp.concatenate([wt[:, :IN_OFFS[2]], kr_group], axis=1)
    uq = (w_uq * (MLA_QK ** -0.5 * LOG2E)).reshape(L, Q_LORA, MLA_HEADS, MLA_QK)
    uq = jnp.pad(uq, ((0, 0), (0, 0), (0, 0), (0, HEAD_PAD - MLA_QK)))
    uk = w_uk.reshape(L, KV_LORA, MLA_HEADS, MLA_NOPE)
    uk = jnp.pad(uk, ((0, 0), (0, 0), (0, 0), (0, HEAD_PAD - MLA_NOPE)))
    uv = w_uv.reshape(L, KV_LORA, MLA_HEADS, MLA_V)
    uv = jnp.pad(uv, ((0, 0), (0, 0), (0, 0), (0, HEAD_PAD - MLA_V)))
    G = MLA_HEADS * HEAD_PAD
    wkt = jnp.swapaxes(w_mem_kv[:, :, :MEM_W], 1, 2) * (MEM_HDIM ** -0.5)
    wv = w_mem_kv[:, :, MEM_W:]
    cast = lambda t: t.astype(BF16)
    return dict(wa=cast(wa), wt=cast(wt),
                wuqt=cast(jnp.swapaxes(uq.reshape(L, Q_LORA, G), 1, 2)),
                wuk=cast(uk.reshape(L, KV_LORA, G)),
                wuvt=cast(jnp.swapaxes(uv.reshape(L, KV_LORA, G), 1, 2)),
                wkt=cast(wkt), wv=cast(wv))


def kernel(x, mem, positions, ffn1_norm, w_ffn1_in, w_ffn1_out, mix_norm, w_in, q_lat_norm, kv_lat_norm,
           w_uq, w_uk, w_uv, w_o_mla, hg_lower_bounds, hg_out_norm, w_o_hg, mem_norm, w_mem_kv, w_o_mem,
           w_out, ffn2_norm, w_ffn2_in, w_ffn2_out, final_norm):
    B, S, D = x.shape
    L = w_in.shape[0]
    T = B * S
    vec = lambda t: t.reshape(L, 1, -1)
    pw = _prep_weights(w_in, w_uq, w_uk, w_uv, w_mem_kv)
    w1a, w1b = w_ffn1_in.astype(BF16), w_ffn1_out.astype(BF16)
    w2a, w2b = w_ffn2_in.astype(BF16), w_ffn2_out.astype(BF16)
    womla, wohg, wout = w_o_mla.astype(BF16), w_o_hg.astype(BF16), w_out.astype(BF16)
    stacked = [vec(mix_norm), pw["wa"], pw["wt"], vec(q_lat_norm), vec(kv_lat_norm),
               pw["wuqt"], pw["wuk"], pw["wuvt"], w_o_mem.astype(BF16)]

    rows, cols = _rope_tables(positions)
    lb, onemlb = _lower_bounds(hg_lower_bounds)
    stacked += [vec(lb), vec(onemlb), vec(hg_out_norm)]
    kt_all, vm_all = _memkv(mem, mem_norm, pw["wkt"], pw["wv"])

    xf = x.reshape(T, D)
    for l in range(L):
        xf = _ffn(xf, vec(ffn1_norm), w1a, w1b, final_norm, l, final_norm=False)
        qt, k, vt, o_hg, g01, pm = _mixer(xf, stacked, l, rows, cols, kt_all, vm_all, seq=S)
        o_mla_t = _attn(qt, k, vt, batch=B, seq=S)
        xf = _merge(xf, o_mla_t, o_hg, g01, pm, womla, wohg, wout, l)
        xf = _ffn(xf, vec(ffn2_norm), w2a, w2b, final_norm, l, final_norm=(l == L - 1))
    return xf.reshape(B, S, D)
```
